```python
import math
import jax, jax.numpy as jnp
from jax import lax
import numpy as np

D_MODEL = 1024
BATCH = 32
SEQ = 2048
DEPTH = 2

HEAD_DIM = 128
HEADS_PER_GROUP = 4
ATTN_PATTERNS = ((128, 1), (512, 4), (2048, 16))
N_ATTN_GROUPS = len(ATTN_PATTERNS)
ATTN_WIDTH = N_ATTN_GROUPS * HEADS_PER_GROUP * HEAD_DIM
ATTN_OUT_WIDTH = HEADS_PER_GROUP * HEAD_DIM
ROPE_DIM = HEAD_DIM // 4
ROPE_THETA = 500000.0
BLOCK = 128
NEG_INF = -1e30
POOL_WINDOWS = (2, 4, 8, 16)
POOL_GROUP_WIDTH = D_MODEL // 4
POOL_WIDTH = len(POOL_WINDOWS) * POOL_GROUP_WIDTH
IN_WIDTH = 3 * ATTN_WIDTH + POOL_WIDTH + 2 * D_MODEL
D_FF = 2816
CONV_WIDTH = 3
PLE_DIM = 256
RMS_EPS = 1e-6

kernel_name = 'hybrid_dilated_attn_pool_gated_merge'


def rmsnorm(x, g):
    x32 = x.astype(jnp.float32)
    y = x32 * lax.rsqrt(jnp.mean(x32 * x32, axis=-1, keepdims=True) + RMS_EPS)
    return (y * g.astype(jnp.float32)).astype(x.dtype)


def partial_rotary(t, cos, sin):
    half = ROPE_DIM // 2
    t1 = t[..., :half].astype(jnp.float32)
    t2 = t[..., half:ROPE_DIM].astype(jnp.float32)
    c = cos[:, None, None, :]
    s = sin[:, None, None, :]
    rot = jnp.concatenate([t1 * c - t2 * s, t2 * c + t1 * s], axis=-1).astype(t.dtype)
    return jnp.concatenate([rot, t[..., ROPE_DIM:]], axis=-1)


def dilated_window_attention(q, k, v, window, dilation):
    B, S, H, hd = q.shape
    span = BLOCK * dilation
    s_pad = -(-S // span) * span
    L = s_pad // dilation
    nb = L // BLOCK
    w_sub = window // dilation

    def to_blocks(t):
        t = jnp.pad(t, ((0, 0), (0, s_pad - S), (0, 0), (0, 0)))
        t = t.reshape(B, L, dilation, H, hd).transpose(0, 2, 1, 3, 4)
        return t.reshape(B, dilation, nb, BLOCK, H, hd)

    def with_prev(t):
        prev = jnp.pad(t, ((0, 0), (0, 0), (1, 0), (0, 0), (0, 0), (0, 0)))[:, :, :-1]
        return jnp.concatenate([prev, t], axis=3)

    qb = to_blocks(q)
    kk = with_prev(to_blocks(k))
    vv = with_prev(to_blocks(v))
    scores = jnp.einsum('brnqhd,brnkhd->brnhqk', qb, kk,
                        preferred_element_type=jnp.float32) * (hd ** -0.5)
    qi = jnp.arange(BLOCK)[:, None]
    ki = jnp.arange(2 * BLOCK)[None, :]
    diff = BLOCK + qi - ki
    band = (diff >= 0) & (diff <= w_sub)
    blk = jnp.arange(nb)[:, None, None]
    mask = band[None] & ((blk > 0) | (ki[None] >= BLOCK))
    scores = jnp.where(mask[None, None, :, None], scores, NEG_INF)
    lse = jax.nn.logsumexp(scores, axis=-1)
    probs = jnp.exp(scores - lse[..., None])
    out = jnp.einsum('brnhqk,brnkhd->brnqhd', probs.astype(v.dtype), vv,
                     preferred_element_type=jnp.float32)
    out = out.reshape(B, dilation, L, H, hd).transpose(0, 2, 1, 3, 4)
    out = out.reshape(B, s_pad, H, hd)[:, :S]
    lse = lse.transpose(0, 1, 2, 4, 3).reshape(B, dilation, L, H).transpose(0, 2, 1, 3)
    lse = lse.reshape(B, s_pad, H)[:, :S]
    return out, lse


def multiscale_pool_mixer(u, pool_w, pool_scale):
    B, S, _ = u.shape
    u32 = u.astype(jnp.float32)
    csum = jnp.cumsum(u32, axis=1)
    t = jnp.arange(S)
    groups = []
    for g, w in enumerate(POOL_WINDOWS):
        sl = slice(g * POOL_GROUP_WIDTH, (g + 1) * POOL_GROUP_WIDTH)
        cg = csum[..., sl]
        shifted = jnp.pad(cg, ((0, 0), (w, 0), (0, 0)))[:, :S]
        count = jnp.minimum(t + 1, w).astype(jnp.float32)
        groups.append((cg - shifted) / count[None, :, None] - u32[..., sl])
    pooled = jnp.stack(groups, axis=2).astype(u.dtype)
    mixed = jnp.einsum('bsgc,gcd->bsgd', pooled, pool_w).reshape(B, S, POOL_WIDTH)
    return mixed * pool_scale


def conv_gated_mlp(h, w_up, conv_w, conv_b, w_down):
    S = h.shape[1]
    u = h @ w_up
    y = conv_b
    for tap in range(CONV_WIDTH):
        shift = CONV_WIDTH - 1 - tap
        y = y + conv_w[tap] * jnp.pad(u, ((0, 0), (shift, 0), (0, 0)))[:, :S]
    gate, val = jnp.split(y, 2, axis=-1)
    return (jax.nn.silu(gate) * val) @ w_down


def setup_inputs(seed: int = 0) -> dict:
    key = jax.random.key(seed)
    ks = jax.random.split(key, 20)
    f32 = jnp.float32

    def nrm(k, shape, fan_in):
        return jax.random.normal(k, shape, f32) * (fan_in ** -0.5)

    def gain(k, shape):
        return 1.0 + 0.02 * jax.random.normal(k, shape, f32)

    return {
        'x': jax.random.normal(ks[0], (BATCH, SEQ, D_MODEL), f32),
        'p': jax.random.normal(ks[1], (DEPTH, BATCH, SEQ, PLE_DIM), f32),
        'g_mix': gain(ks[2], (DEPTH, D_MODEL)),
        'w_in': nrm(ks[3], (DEPTH, D_MODEL, IN_WIDTH), D_MODEL),
        'w_ya': nrm(ks[4], (DEPTH, ATTN_OUT_WIDTH, D_MODEL), ATTN_OUT_WIDTH),
        'w_yb': nrm(ks[5], (DEPTH, POOL_WIDTH, D_MODEL), POOL_WIDTH),
        'pool_w': nrm(ks[6], (DEPTH, len(POOL_WINDOWS), POOL_GROUP_WIDTH, POOL_GROUP_WIDTH), POOL_GROUP_WIDTH),
        'pool_scale': gain(ks[7], (DEPTH, POOL_WIDTH)),
        'w_o': nrm(ks[8], (DEPTH, D_MODEL, D_MODEL), D_MODEL),
        'g_ffn': gain(ks[9], (DEPTH, D_MODEL)),
        'w_up': nrm(ks[10], (DEPTH, D_MODEL, 2 * D_FF), D_MODEL),
        'conv_w': nrm(ks[11], (DEPTH, CONV_WIDTH, 2 * D_FF), CONV_WIDTH),
        'conv_b': 0.01 * jax.random.normal(ks[12], (DEPTH, 2 * D_FF), f32),
        'w_down': nrm(ks[13], (DEPTH, D_FF, D_MODEL), D_FF),
        'g_ple': gain(ks[14], (DEPTH, D_MODEL)),
        'w_ple': nrm(ks[15], (DEPTH, PLE_DIM, D_MODEL), PLE_DIM),
        'w_ple_gate': nrm(ks[16], (DEPTH, D_MODEL, D_MODEL), D_MODEL),
        'g_final': gain(ks[17], (D_MODEL,)),
    }


def reference(x, p, g_mix, w_in, w_ya, w_yb, pool_w, pool_scale, w_o, g_ffn,
              w_up, conv_w, conv_b, w_down, g_ple, w_ple, w_ple_gate, g_final):
    B, S, _ = x.shape
    pos = jnp.arange(S, dtype=jnp.float32)
    inv_freq = jnp.exp(jnp.arange(0, ROPE_DIM, 2, dtype=jnp.float32)
                       * (-math.log(ROPE_THETA) / ROPE_DIM))
    ang = pos[:, None] * inv_freq[None, :]
    cos, sin = jnp.cos(ang), jnp.sin(ang)
    split_at = [ATTN_WIDTH, 2 * ATTN_WIDTH, 3 * ATTN_WIDTH,
                3 * ATTN_WIDTH + POOL_WIDTH, 3 * ATTN_WIDTH + POOL_WIDTH + D_MODEL]
    head_shape = (B, S, N_ATTN_GROUPS, HEADS_PER_GROUP, HEAD_DIM)

    for i in range(DEPTH):
        h = rmsnorm(x, g_mix[i])
        z = h @ w_in[i]
        q, k, v, u_pool, gate_a, gate_b = jnp.split(z, split_at, axis=-1)
        q = partial_rotary(q.reshape(head_shape), cos, sin)
        k = partial_rotary(k.reshape(head_shape), cos, sin)
        v = v.reshape(head_shape)

        outs, lses = [], []
        for g, (window, dilation) in enumerate(ATTN_PATTERNS):
            o_g, lse_g = dilated_window_attention(q[:, :, g], k[:, :, g], v[:, :, g],
                                                  window, dilation)
            outs.append(o_g)
            lses.append(lse_g)
        weights = jax.nn.softmax(jnp.stack(lses, axis=0), axis=0)
        attn = jnp.sum(weights[..., None] * jnp.stack(outs, axis=0), axis=0)
        y_a = attn.reshape(B, S, ATTN_OUT_WIDTH).astype(x.dtype) @ w_ya[i]

        y_b = multiscale_pool_mixer(u_pool, pool_w[i], pool_scale[i]) @ w_yb[i]

        merged = jax.nn.sigmoid(gate_a) * y_a + jax.nn.sigmoid(gate_b) * y_b
        x = x + merged @ w_o[i]

        x = x + conv_gated_mlp(rmsnorm(x, g_ffn[i]), w_up[i], conv_w[i], conv_b[i], w_down[i])

        ple_gate = jax.nn.sigmoid(rmsnorm(x, g_ple[i]) @ w_ple_gate[i])
        x = x + (p[i] @ w_ple[i]) * ple_gate

    return rmsnorm(x, g_final)
```

```python
import functools
import math

import jax
import jax.numpy as jnp
from jax import lax
from jax.experimental import pallas as pl
from jax.experimental.pallas import tpu as pltpu

F32 = jnp.float32
BF16 = jnp.bfloat16

HEAD_DIM = 128
HEADS_PER_GROUP = 4
GROUP_WIDTH = HEADS_PER_GROUP * HEAD_DIM
ATTN_PATTERNS = ((128, 1), (512, 4), (2048, 16))
N_GROUPS = len(ATTN_PATTERNS)
ATTN_WIDTH = N_GROUPS * GROUP_WIDTH
ROPE_DIM = HEAD_DIM // 4
ROPE_HALF = ROPE_DIM // 2
ROPE_THETA = 500000.0
BLOCK = 128
NEG_INF = -1e30
POOL_WINDOWS = (2, 4, 8, 16)
POOL_GROUP_WIDTH = 256
POOL_WIDTH = len(POOL_WINDOWS) * POOL_GROUP_WIDTH
POOL_HALO = 16
CONV_WIDTH = 3
CONV_HALO = 8
RMS_EPS = 1e-6

VMEM_LIMIT_BYTES = 56 * 1024 * 1024
INPROJ_TOKENS = 512
POST_TOKENS = 256
FF_CHUNK = 256


def _rms_scale(x, g):
    ms = jnp.mean(x * x, axis=-1, keepdims=True)
    return x * lax.rsqrt(ms + RMS_EPS) * g


def _resident(shape):
    zeros = (0,) * len(shape)
    return pl.BlockSpec(shape, lambda *_: zeros, pipeline_mode=pl.Buffered(1))


def _inproj_kernel(x_ref, g_ref, w_ref, qc_ref, qs1_ref, qs2_ref, kc_ref, ks1_ref, ks2_ref,
                   q0_ref, k0_ref, v0_ref, q1_ref, k1_ref, v1_ref, q2_ref, k2_ref, v2_ref,
                   pool_ref, ga_ref, gb_ref, perm_ref, *, tm):
    h = _rms_scale(x_ref[0], g_ref[...]).astype(BF16)

    def proj(col0, width):
        return jnp.dot(h, w_ref[:, col0:col0 + width], preferred_element_type=F32)

    def rotary(t, c_ref, s1_ref, s2_ref):
        c, s1, s2 = c_ref[...], s1_ref[...], s2_ref[...]
        heads = []
        for hh in range(HEADS_PER_GROUP):
            th = t[:, hh * HEAD_DIM:(hh + 1) * HEAD_DIM]
            heads.append(th * c + pltpu.roll(th, HEAD_DIM - ROPE_HALF, 1) * s1
                         + pltpu.roll(th, ROPE_HALF, 1) * s2)
        return jnp.concatenate(heads, axis=1)

    slot = [0]

    def put(dst_ref, val, d):
        if d == 1:
            dst_ref[0] = val.astype(BF16)
            return
        for hh in range(HEADS_PER_GROUP):
            lanes = slice(hh * HEAD_DIM, (hh + 1) * HEAD_DIM)
            scr = perm_ref.at[slot[0]]
            slot[0] += 1
            scr[...] = val[:, lanes]
            for r in range(d):
                dst_ref[0, r, :, lanes] = scr[pl.ds(r, tm // d, stride=d), :].astype(BF16)

    outs = ((q0_ref, k0_ref, v0_ref), (q1_ref, k1_ref, v1_ref), (q2_ref, k2_ref, v2_ref))
    for g, (_, d) in enumerate(ATTN_PATTERNS):
        q = proj(g * GROUP_WIDTH, GROUP_WIDTH)
        put(outs[g][0], rotary(q, qc_ref, qs1_ref, qs2_ref), d)
        k = proj(ATTN_WIDTH + g * GROUP_WIDTH, GROUP_WIDTH)
        put(outs[g][1], rotary(k, kc_ref, ks1_ref, ks2_ref), d)
        v = proj(2 * ATTN_WIDTH + g * GROUP_WIDTH, GROUP_WIDTH)
        put(outs[g][2], v, d)
    base = 3 * ATTN_WIDTH
    d_model = ga_ref.shape[-1]
    pool_ref[0] = proj(base, POOL_WIDTH).astype(BF16)
    ga_ref[0] = proj(base + POOL_WIDTH, d_model).astype(BF16)
    gb_ref[0] = proj(base + POOL_WIDTH + d_model, d_model).astype(BF16)


def _inproj(x, g, w_in, tables, tm):
    B, S, D = x.shape
    n_in = w_in.shape[1]
    grid = (B, S // tm)
    tok = lambda width: pl.BlockSpec((1, tm, width), lambda b, i: (b, i, 0))
    tab = pl.BlockSpec((tm, HEAD_DIM), lambda b, i: (i, 0))
    out_shape, out_specs = [], []
    for _, d in ATTN_PATTERNS:
        for _ in range(3):
            if d == 1:
                out_shape.append(jax.ShapeDtypeStruct((B, S, GROUP_WIDTH), BF16))
                out_specs.append(tok(GROUP_WIDTH))
            else:
                out_shape.append(jax.ShapeDtypeStruct((B, d, S // d, GROUP_WIDTH), BF16))
                out_specs.append(pl.BlockSpec((1, d, tm // d, GROUP_WIDTH), lambda b, i: (b, 0, i, 0)))
    for width in (POOL_WIDTH, D, D):
        out_shape.append(jax.ShapeDtypeStruct((B, S, width), BF16))
        out_specs.append(tok(width))
    n_perm = 3 * HEADS_PER_GROUP * sum(1 for _, d in ATTN_PATTERNS if d > 1)
    return pl.pallas_call(
        functools.partial(_inproj_kernel, tm=tm),
        grid=grid,
        in_specs=[tok(D), _resident((1, D)), _resident((D, n_in))] + [tab] * 6,
        out_specs=out_specs,
        out_shape=out_shape,
        scratch_shapes=[pltpu.VMEM((n_perm, tm, HEAD_DIM), F32)],
        compiler_params=pltpu.CompilerParams(
            dimension_semantics=("arbitrary", "arbitrary"), vmem_limit_bytes=VMEM_LIMIT_BYTES),
        name="inproj",
    )(x, g, w_in, *tables)


def _attn_kernel(*args, seq):
    n_groups = len(ATTN_PATTERNS)
    refs = [args[3 * g:3 * g + 3] for g in range(n_groups)]
    o_ref, acc_ref, m_ref, l_ref = args[3 * n_groups:]
    row = lax.broadcasted_iota(jnp.int32, (BLOCK, 2 * BLOCK), 0)
    col = lax.broadcasted_iota(jnp.int32, (BLOCK, 2 * BLOCK), 1)
    bias_two = jnp.where(col < BLOCK, jnp.where(col >= row, 0.0, NEG_INF),
                         jnp.where(col - BLOCK <= row, 0.0, NEG_INF)).astype(F32)
    row1 = lax.broadcasted_iota(jnp.int32, (BLOCK, BLOCK), 0)
    col1 = lax.broadcasted_iota(jnp.int32, (BLOCK, BLOCK), 1)
    bias_one = jnp.where(col1 <= row1, 0.0, NEG_INF).astype(F32)
    ones = jnp.ones((2 * BLOCK, HEAD_DIM), BF16)

    def block(qb, kk, vv, bias):
        s = lax.dot_general(qb, kk, (((1,), (1,)), ((), ())), preferred_element_type=F32) + bias
        m = jnp.max(s, axis=-1, keepdims=True)
        p = jnp.exp(s - m).astype(BF16)
        vext = jnp.concatenate([vv, ones[:vv.shape[0]]], axis=1)
        o = jnp.dot(p, vext, preferred_element_type=F32)
        return o[:, :HEAD_DIM], o[:, HEAD_DIM:], jnp.broadcast_to(m, (BLOCK, HEAD_DIM))

    for g, (_, d) in enumerate(ATTN_PATTERNS):
        qr, kr, vr = refs[g]
        nb = seq // d // BLOCK
        for r in range(d):
            def ld(ref, lo, size):
                return ref[0, lo:lo + size, :] if d == 1 else ref[0, r, lo:lo + size, :]
            for n in range(nb):
                qb = ld(qr, n * BLOCK, BLOCK)
                if n == 0:
                    o, l, m = block(qb, ld(kr, 0, BLOCK), ld(vr, 0, BLOCK), bias_one)
                else:
                    o, l, m = block(qb, ld(kr, (n - 1) * BLOCK, 2 * BLOCK),
                                    ld(vr, (n - 1) * BLOCK, 2 * BLOCK), bias_two)
                rows = pl.ds(n * BLOCK, BLOCK) if d == 1 else pl.ds(n * BLOCK * d + r, BLOCK, stride=d)
                acc_ref[g, rows, :] = o
                l_ref[g, rows, :] = l
                m_ref[g, rows, :] = m

    for c in range(seq // BLOCK):
        rows = pl.ds(c * BLOCK, BLOCK)
        ms = [m_ref[g, rows, :] for g in range(n_groups)]
        mx = functools.reduce(jnp.maximum, ms)
        ws = [jnp.exp(m - mx) for m in ms]
        num = sum(w * acc_ref[g, rows, :] for g, w in enumerate(ws))
        den = sum(w * l_ref[g, rows, :] for g, w in enumerate(ws))
        o_ref[0, rows, :] = (num / den).astype(BF16)


def _attention(qkv, seq):
    B = qkv[0].shape[0]
    in_specs = []
    for _, d in ATTN_PATTERNS:
        for _ in range(3):
            if d == 1:
                in_specs.append(pl.BlockSpec((1, seq, HEAD_DIM), lambda b, h: (b, 0, h)))
            else:
                in_specs.append(pl.BlockSpec((1, d, seq // d, HEAD_DIM), lambda b, h: (b, 0, 0, h)))
    return pl.pallas_call(
        functools.partial(_attn_kernel, seq=seq),
        grid=(B, HEADS_PER_GROUP),
        in_specs=in_specs,
        out_specs=pl.BlockSpec((1, seq, HEAD_DIM), lambda b, h: (b, 0, h)),
        out_shape=jax.ShapeDtypeStruct((B, seq, GROUP_WIDTH), BF16),
        scratch_shapes=[pltpu.VMEM((len(ATTN_PATTERNS), seq, HEAD_DIM), F32)] * 3,
        compiler_params=pltpu.CompilerParams(
            dimension_semantics=("arbitrary", "arbitrary"), vmem_limit_bytes=VMEM_LIMIT_BYTES),
        name="attention",
    )(*qkv)


def _post_kernel(x_ref, attn_ref, pool_ref, prev_ref, ga_ref, gb_ref, p_ref,
                 wya_ref, poolw_ref, pscale_ref, wyb_ref, wo_ref, gffn_ref, wup_ref, cw_ref, cb_ref,
                 wdown_ref, gple_ref, wple_ref, wpleg_ref, gfin_ref, out_ref, u_scr, *, tm, final):
    i = pl.program_id(1)
    d_ff = wdown_ref.shape[0]
    x = x_ref[0]

    y_a = jnp.dot(attn_ref[0], wya_ref[...], preferred_element_type=F32)

    u = pool_ref[0].astype(F32)
    prev = jnp.where(i > 0, prev_ref[0].astype(F32), 0.0)
    tok = i * tm + lax.broadcasted_iota(jnp.int32, (tm, 1), 0)
    mixed = []
    for g, w in enumerate(POOL_WINDOWS):
        cols = slice(g * POOL_GROUP_WIDTH, (g + 1) * POOL_GROUP_WIDTH)
        ug = u[:, cols]
        s = jnp.concatenate([prev[:, cols], ug], axis=0)
        shift = 1
        while shift < w:
            s = s + pltpu.roll(s, shift, 0)
            shift *= 2
        count = jnp.minimum(tok + 1, w).astype(F32)
        pooled = s[POOL_HALO:] / count - ug
        mixed.append(jnp.dot(pooled.astype(BF16), poolw_ref[g], preferred_element_type=F32))
    mixed = jnp.concatenate(mixed, axis=1) * pscale_ref[...]
    y_b = jnp.dot(mixed.astype(BF16), wyb_ref[...], preferred_element_type=F32)

    merged = (jax.nn.sigmoid(ga_ref[0].astype(F32)) * y_a
              + jax.nn.sigmoid(gb_ref[0].astype(F32)) * y_b)
    x = x + jnp.dot(merged.astype(BF16), wo_ref[...], preferred_element_type=F32)

    @pl.when(i == 0)
    def _():
        u_scr[0:CONV_HALO, :] = jnp.zeros((CONV_HALO, u_scr.shape[1]), F32)

    h = _rms_scale(x, gffn_ref[...]).astype(BF16)
    u_scr[CONV_HALO:CONV_HALO + tm, :] = jnp.dot(h, wup_ref[...], preferred_element_type=F32)

    def conv(cols):
        y = cb_ref[:, cols]
        for tap in range(CONV_WIDTH):
            lo = CONV_HALO - (CONV_WIDTH - 1 - tap)
            y = y + cw_ref[tap:tap + 1, cols] * u_scr[lo:lo + tm, cols]
        return y

    acts = []
    for j in range(d_ff // FF_CHUNK):
        gate = conv(slice(j * FF_CHUNK, (j + 1) * FF_CHUNK))
        val = conv(slice(d_ff + j * FF_CHUNK, d_ff + (j + 1) * FF_CHUNK))
        acts.append((gate * jax.nn.sigmoid(gate) * val).astype(BF16))
    u_scr[0:CONV_HALO, :] = u_scr[tm:tm + CONV_HALO, :]
    x = x + jnp.dot(jnp.concatenate(acts, axis=1), wdown_ref[...], preferred_element_type=F32)

    h = _rms_scale(x, gple_ref[...]).astype(BF16)
    gate = jax.nn.sigmoid(jnp.dot(h, wpleg_ref[...], preferred_element_type=F32))
    emb = jnp.dot(p_ref[0].astype(BF16), wple_ref[...], preferred_element_type=F32)
    x = x + emb * gate
    if final:
        x = _rms_scale(x, gfin_ref[...])
    out_ref[0] = x


def _post(x, attn, pool, ga, gb, p, w, tm, final):
    B, S, D = x.shape
    assert S % tm == 0 and tm % POOL_HALO == 0
    tok = lambda width: pl.BlockSpec((1, tm, width), lambda b, i: (b, i, 0))
    prev = pl.BlockSpec((1, POOL_HALO, POOL_WIDTH),
                        lambda b, i: (b, jnp.maximum(i * (tm // POOL_HALO) - 1, 0), 0))
    weights = (w["w_ya"], w["pool_w"], w["pool_scale"], w["w_yb"], w["w_o"], w["g_ffn"], w["w_up"],
               w["conv_w"], w["conv_b"], w["w_down"], w["g_ple"], w["w_ple"], w["w_ple_gate"], w["g_final"])
    return pl.pallas_call(
        functools.partial(_post_kernel, tm=tm, final=final),
        grid=(B, S // tm),
        in_specs=[tok(D), tok(attn.shape[-1]), tok(POOL_WIDTH), prev, tok(D), tok(D), tok(p.shape[-1])]
        + [_resident(a.shape) for a in weights],
        out_specs=tok(D),
        out_shape=jax.ShapeDtypeStruct((B, S, D), F32),
        scratch_shapes=[pltpu.VMEM((tm + CONV_HALO, w["w_up"].shape[1]), F32)],
        compiler_params=pltpu.CompilerParams(
            dimension_semantics=("arbitrary", "arbitrary"), vmem_limit_bytes=VMEM_LIMIT_BYTES),
        name="post",
    )(x, attn, pool, pool, ga, gb, p, *weights)


def _rotary_tables(seq):
    pos = jnp.arange(seq, dtype=F32)
    inv_freq = jnp.exp(jnp.arange(0, ROPE_DIM, 2, dtype=F32) * (-math.log(ROPE_THETA) / ROPE_DIM))
    ang = pos[:, None] * inv_freq[None, :]
    cos, sin = jnp.cos(ang), jnp.sin(ang)
    rest = HEAD_DIM - ROPE_DIM
    c = jnp.concatenate([cos, cos, jnp.ones((seq, rest), F32)], axis=1)
    s1 = jnp.concatenate([-sin, jnp.zeros((seq, HEAD_DIM - ROPE_HALF), F32)], axis=1)
    s2 = jnp.concatenate([jnp.zeros((seq, ROPE_HALF), F32), sin, jnp.zeros((seq, rest), F32)], axis=1)
    scale = HEAD_DIM ** -0.5
    return (c * scale, s1 * scale, s2 * scale, c, s1, s2)


def kernel(x, p, g_mix, w_in, w_ya, w_yb, pool_w, pool_scale, w_o, g_ffn, w_up, conv_w, conv_b,
           w_down, g_ple, w_ple, w_ple_gate, g_final):
    B, S, D = x.shape
    depth = w_in.shape[0]
    for window, d in ATTN_PATTERNS:
        assert window // d == BLOCK and S % (BLOCK * d) == 0
    assert S % INPROJ_TOKENS == 0
    tables = _rotary_tables(S)
    row = lambda a: a.reshape(1, -1)
    for i in range(depth):
        z = _inproj(x, row(g_mix[i]), w_in[i].astype(BF16), tables, INPROJ_TOKENS)
        attn = _attention(z[:9], S)
        pool, ga, gb = z[9:]
        weights = dict(
            w_ya=w_ya[i].astype(BF16), pool_w=pool_w[i].astype(BF16), pool_scale=row(pool_scale[i]),
            w_yb=w_yb[i].astype(BF16), w_o=w_o[i].astype(BF16), g_ffn=row(g_ffn[i]),
            w_up=w_up[i].astype(BF16), conv_w=conv_w[i], conv_b=row(conv_b[i]),
            w_down=w_down[i].astype(BF16), g_ple=row(g_ple[i]), w_ple=w_ple[i].astype(BF16),
            w_ple_gate=w_ple_gate[i].astype(BF16), g_final=row(g_final))
        x = _post(x, attn, pool, ga, gb, p[i], weights, POST_TOKENS, final=(i == depth - 1))
    return x
```

```python
import functools
import math

import jax
import jax.numpy as jnp
from jax import lax
from jax.experimental import pallas as pl
from jax.experimental.pallas import tpu as pltpu

F32 = jnp.float32
BF16 = jnp.bfloat16

HEAD_DIM = 128
HEADS_PER_GROUP = 4
GROUP_WIDTH = HEADS_PER_GROUP * HEAD_DIM
ATTN_PATTERNS = ((128, 1), (512, 4), (2048, 16))
ATTN_WIDTH = len(ATTN_PATTERNS) * GROUP_WIDTH
ROPE_DIM = HEAD_DIM // 4
ROPE_HALF = ROPE_DIM // 2
ROPE_THETA = 500000.0
BLOCK = 128
NEG_INF = -1e30
POOL_WINDOWS = (2, 4, 8, 16)
POOL_GROUP_WIDTH = 256
POOL_WIDTH = len(POOL_WINDOWS) * POOL_GROUP_WIDTH
POOL_HALO = 32
LANES = 128
CONV_WIDTH = 3
CONV_HALO = 8
RMS_EPS = 1e-6

VMEM_LIMIT_BYTES = 58 * 1024 * 1024
INPROJ_TOKENS = 512
POST_TOKENS = 512
POST_SUB_TOKENS = 256
FF_CHUNK = 256
CONV_SLABS = 8

_PARAMS = pltpu.CompilerParams(
    dimension_semantics=("arbitrary", "arbitrary"), vmem_limit_bytes=VMEM_LIMIT_BYTES)


def _rms_scale(x, g):
    ms = jnp.mean(x * x, axis=-1, keepdims=True)
    return x * lax.rsqrt(ms + RMS_EPS) * g


def _layer_block(arr, layer):
    tail = (0,) * (arr.ndim - 1)
    return pl.BlockSpec((None,) + arr.shape[1:], lambda *_: (layer,) + tail,
                        pipeline_mode=pl.Buffered(1))


def _dot(a, b):
    return jnp.dot(a, b, preferred_element_type=F32)


def _inproj_kernel(x_ref, g_ref, w_ref, qc_ref, qs1_ref, qs2_ref, kc_ref, ks1_ref, ks2_ref,
                   q0_ref, k0_ref, v0_ref, q1_ref, k1_ref, v1_ref, q2_ref, k2_ref, v2_ref,
                   pool_ref, ga_ref, gb_ref, h_scr, perm_scr, *, tm):
    h_scr[...] = _rms_scale(x_ref[...], g_ref[...]).astype(BF16)

    def proj(col0, width):
        return _dot(h_scr[...], w_ref[:, col0:col0 + width])

    def rotary(t, c_ref, s1_ref, s2_ref):
        c, s1, s2 = c_ref[...], s1_ref[...], s2_ref[...]
        heads = []
        for hh in range(HEADS_PER_GROUP):
            th = t[:, hh * HEAD_DIM:(hh + 1) * HEAD_DIM]
            heads.append(th * c + pltpu.roll(th, HEAD_DIM - ROPE_HALF, 1) * s1
                         + pltpu.roll(th, ROPE_HALF, 1) * s2)
        return jnp.concatenate(heads, axis=1)

    slot = [0]

    def put(dst_ref, val, d):
        for hh in range(HEADS_PER_GROUP):
            head = val[:, hh * HEAD_DIM:(hh + 1) * HEAD_DIM]
            if d == 1:
                dst_ref[hh] = head.astype(BF16)
                continue
            scr = perm_scr.at[slot[0]]
            slot[0] += 1
            scr[...] = head
            for r in range(d):
                dst_ref[hh, r] = scr[pl.ds(r, tm // d, stride=d), :].astype(BF16)

    outs = ((q0_ref, k0_ref, v0_ref), (q1_ref, k1_ref, v1_ref), (q2_ref, k2_ref, v2_ref))
    for g, (_, d) in enumerate(ATTN_PATTERNS):
        q = proj(g * GROUP_WIDTH, GROUP_WIDTH)
        put(outs[g][0], rotary(q, qc_ref, qs1_ref, qs2_ref), d)
        k = proj(ATTN_WIDTH + g * GROUP_WIDTH, GROUP_WIDTH)
        put(outs[g][1], rotary(k, kc_ref, ks1_ref, ks2_ref), d)
        v = proj(2 * ATTN_WIDTH + g * GROUP_WIDTH, GROUP_WIDTH)
        put(outs[g][2], v, d)
    base = 3 * ATTN_WIDTH
    d_model = ga_ref.shape[-1]
    pool_ref[...] = proj(base, POOL_WIDTH).astype(BF16)
    ga_ref[...] = proj(base + POOL_WIDTH, d_model).astype(BF16)
    gb_ref[...] = proj(base + POOL_WIDTH + d_model, d_model).astype(BF16)


def _inproj(x, g_mix, w_in, tables, layer, tm):
    B, S, D = x.shape
    tok = lambda width: pl.BlockSpec((None, tm, width), lambda b, i: (b, i, 0))
    tab = pl.BlockSpec((tm, HEAD_DIM), lambda b, i: (i, 0))
    out_shape, out_specs = [], []
    for _, d in ATTN_PATTERNS:
        for _ in range(3):
            if d == 1:
                out_shape.append(jax.ShapeDtypeStruct((B, HEADS_PER_GROUP, S, HEAD_DIM), BF16))
                out_specs.append(pl.BlockSpec((None, HEADS_PER_GROUP, tm, HEAD_DIM),
                                              lambda b, i: (b, 0, i, 0)))
            else:
                out_shape.append(jax.ShapeDtypeStruct((B, HEADS_PER_GROUP, d, S // d, HEAD_DIM), BF16))
                out_specs.append(pl.BlockSpec((None, HEADS_PER_GROUP, d, tm // d, HEAD_DIM),
                                              lambda b, i: (b, 0, 0, i, 0)))
    for width in (POOL_WIDTH, D, D):
        out_shape.append(jax.ShapeDtypeStruct((B, S, width), BF16))
        out_specs.append(tok(width))
    n_perm = 3 * HEADS_PER_GROUP * sum(1 for _, d in ATTN_PATTERNS if d > 1)
    return pl.pallas_call(
        functools.partial(_inproj_kernel, tm=tm),
        grid=(B, S // tm),
        in_specs=[tok(D), _layer_block(g_mix, layer), _layer_block(w_in, layer)] + [tab] * 6,
        out_specs=out_specs,
        out_shape=out_shape,
        scratch_shapes=[pltpu.VMEM((tm, D), BF16), pltpu.VMEM((n_perm, tm, HEAD_DIM), F32)],
        compiler_params=_PARAMS,
        name="inproj",
    )(x, g_mix, w_in, *tables)


def _attn_kernel(*args, seq):
    n_groups = len(ATTN_PATTERNS)
    refs = [args[3 * g:3 * g + 3] for g in range(n_groups)]
    o_ref, acc_ref, m_ref, l_ref = args[3 * n_groups:]
    row = lax.broadcasted_iota(jnp.int32, (BLOCK, 2 * BLOCK), 0)
    col = lax.broadcasted_iota(jnp.int32, (BLOCK, 2 * BLOCK), 1)
    bias_two = jnp.where(col < BLOCK, jnp.where(col >= row, 0.0, NEG_INF),
                         jnp.where(col - BLOCK <= row, 0.0, NEG_INF)).astype(F32)
    row1 = lax.broadcasted_iota(jnp.int32, (BLOCK, BLOCK), 0)
    col1 = lax.broadcasted_iota(jnp.int32, (BLOCK, BLOCK), 1)
    bias_one = jnp.where(col1 <= row1, 0.0, NEG_INF).astype(F32)

    def block(ld, n):
        qr, kr, vr = ld
        qb = qr(n * BLOCK, BLOCK)
        lo, size, bias = (0, BLOCK, bias_one) if n == 0 else ((n - 1) * BLOCK, 2 * BLOCK, bias_two)
        s = lax.dot_general(qb, kr(lo, size), (((1,), (1,)), ((), ())),
                            preferred_element_type=F32) + bias
        m = jnp.max(s, axis=-1, keepdims=True)
        p = jnp.exp(s - m).astype(BF16)
        vext = jnp.concatenate([vr(lo, size), jnp.ones((size, HEAD_DIM), BF16)], axis=1)
        o = _dot(p, vext)
        return o[:, :HEAD_DIM], o[:, HEAD_DIM:], jnp.broadcast_to(m, (BLOCK, HEAD_DIM))

    for g, (_, d) in enumerate(ATTN_PATTERNS):
        if d == 1:
            continue
        for r in range(d):
            ld = [lambda lo, size, ref=ref: ref[r, lo:lo + size, :] for ref in refs[g]]
            for n in range(seq // d // BLOCK):
                o, l, m = block(ld, n)
                rows = pl.ds(n * BLOCK * d + r, BLOCK, stride=d)
                acc_ref[g - 1, rows, :] = o
                l_ref[g - 1, rows, :] = l
                m_ref[g - 1, rows, :] = m

    ld = [lambda lo, size, ref=ref: ref[lo:lo + size, :] for ref in refs[0]]
    for n in range(seq // BLOCK):
        rows = pl.ds(n * BLOCK, BLOCK)
        o, l, m = block(ld, n)
        os_ = [o] + [acc_ref[g, rows, :] for g in range(n_groups - 1)]
        ls = [l] + [l_ref[g, rows, :] for g in range(n_groups - 1)]
        ms = [m] + [m_ref[g, rows, :] for g in range(n_groups - 1)]
        mx = functools.reduce(jnp.maximum, ms)
        ws = [jnp.exp(mi - mx) for mi in ms]
        num = sum(w * oi for w, oi in zip(ws, os_))
        den = sum(w * li for w, li in zip(ws, ls))
        o_ref[rows, :] = (num / den).astype(BF16)


def _attention(qkv, seq):
    assert ATTN_PATTERNS[0][1] == 1 and all(d > 1 for _, d in ATTN_PATTERNS[1:])
    B = qkv[0].shape[0]
    in_specs = []
    for _, d in ATTN_PATTERNS:
        for _ in range(3):
            if d == 1:
                in_specs.append(pl.BlockSpec((None, None, seq, HEAD_DIM), lambda b, h: (b, h, 0, 0)))
            else:
                in_specs.append(pl.BlockSpec((None, None, d, seq // d, HEAD_DIM),
                                             lambda b, h: (b, h, 0, 0, 0)))
    n_dilated = len(ATTN_PATTERNS) - 1
    return pl.pallas_call(
        functools.partial(_attn_kernel, seq=seq),
        grid=(B, HEADS_PER_GROUP),
        in_specs=in_specs,
        out_specs=pl.BlockSpec((None, None, seq, HEAD_DIM), lambda b, h: (b, h, 0, 0)),
        out_shape=jax.ShapeDtypeStruct((B, HEADS_PER_GROUP, seq, HEAD_DIM), BF16),
        scratch_shapes=[pltpu.VMEM((n_dilated, seq, HEAD_DIM), F32)] * 3,
        compiler_params=_PARAMS,
        name="attention",
    )(*qkv)


def _post_kernel(x_ref, attn_ref, pool_ref, prev_ref, ga_ref, gb_ref, p_ref,
                 wya_ref, poolw_ref, pscale_ref, wyb_ref, wo_ref, gffn_ref, wup_ref, cw_ref, cb_ref,
                 wdown_ref, gple_ref, wple_ref, wpleg_ref, gfin_ref, out_ref,
                 h_scr, pool_scr, conv_scr, halo_scr, act_scr, *, tm, sub, final):
    i = pl.program_id(1)
    d_ff = wdown_ref.shape[0]
    ring = [0]

    @pl.when(i == 0)
    def _():
        halo_scr[...] = jnp.zeros(halo_scr.shape, F32)

    def chain(c):
        r0 = c * sub
        rows = slice(r0, r0 + sub)
        x = x_ref[rows, :]

        attn = jnp.concatenate([attn_ref[hh, rows, :] for hh in range(HEADS_PER_GROUP)], axis=1)
        y_a = _dot(attn, wya_ref[...])

        u = pool_ref[rows, :].astype(F32)
        if c == 0:
            prev = jnp.where(i > 0, prev_ref[...].astype(F32), 0.0)
        else:
            prev = pool_ref[r0 - POOL_HALO:r0, :].astype(F32)
        tok = i * tm + r0 + lax.broadcasted_iota(jnp.int32, (sub, 1), 0)
        end = POOL_HALO + sub
        mixed = []
        for g, w in enumerate(POOL_WINDOWS):
            count = jnp.minimum(tok + 1, w).astype(F32)
            pooled = []
            for k in range(POOL_GROUP_WIDTH // LANES):
                slab = g * (POOL_GROUP_WIDTH // LANES) + k
                lanes = slice(slab * LANES, (slab + 1) * LANES)
                cur, nxt = pool_scr.at[0, slab], pool_scr.at[1, slab]
                cur[0:POOL_HALO, :] = prev[:, lanes]
                cur[POOL_HALO:end, :] = u[:, lanes]
                lo, shift = 0, 1
                while shift < w:
                    lo += 8
                    nxt[lo:end, :] = cur[lo:end, :] + cur[lo - shift:end - shift, :]
                    cur, nxt = nxt, cur
                    shift *= 2
                pooled.append(cur[POOL_HALO:end, :] / count - u[:, lanes])
            pooled = jnp.concatenate(pooled, axis=1)
            mixed.append(_dot(pooled.astype(BF16), poolw_ref[g]))
        mixed = jnp.concatenate(mixed, axis=1) * pscale_ref[...]
        y_b = _dot(mixed.astype(BF16), wyb_ref[...])

        merged = (jax.nn.sigmoid(ga_ref[rows, :].astype(F32)) * y_a
                  + jax.nn.sigmoid(gb_ref[rows, :].astype(F32)) * y_b)
        x = x + _dot(merged.astype(BF16), wo_ref[...])

        h_scr[rows, :] = _rms_scale(x, gffn_ref[...]).astype(BF16)
        for j in range(d_ff // FF_CHUNK):
            acts = []
            for half in range(2):
                col0 = half * d_ff + j * FF_CHUNK
                up = _dot(h_scr[rows, :], wup_ref[:, col0:col0 + FF_CHUNK])
                ys = []
                for k in range(FF_CHUNK // LANES):
                    cols = slice(col0 + k * LANES, col0 + (k + 1) * LANES)
                    buf = conv_scr.at[ring[0] % conv_scr.shape[0]]
                    ring[0] += 1
                    buf[0:CONV_HALO, :] = halo_scr[:, cols]
                    buf[CONV_HALO:CONV_HALO + sub, :] = up[:, k * LANES:(k + 1) * LANES]
                    y = cb_ref[:, cols]
                    for tap in range(CONV_WIDTH):
                        lo = CONV_HALO - (CONV_WIDTH - 1 - tap)
                        y = y + cw_ref[tap:tap + 1, cols] * buf[lo:lo + sub, :]
                    halo_scr[:, cols] = buf[sub:sub + CONV_HALO, :]
                    ys.append(y)
                acts.append(jnp.concatenate(ys, axis=1))
            gate, val = acts
            act_scr[rows, j * FF_CHUNK:(j + 1) * FF_CHUNK] = (
                gate * jax.nn.sigmoid(gate) * val).astype(BF16)
        x = x + _dot(act_scr[rows, :], wdown_ref[...])

        h = _rms_scale(x, gple_ref[...]).astype(BF16)
        gate = jax.nn.sigmoid(_dot(h, wpleg_ref[...]))
        x = x + _dot(p_ref[rows, :].astype(BF16), wple_ref[...]) * gate
        if final:
            x = _rms_scale(x, gfin_ref[...])
        out_ref[rows, :] = x

    for c in range(tm // sub):
        chain(c)


def _post(x, attn, pool, ga, gb, p, weights, g_final, layer, tm, final):
    B, S, D = x.shape
    assert S % tm == 0 and tm % POOL_HALO == 0
    d_up = weights[6].shape[-1]
    tok = lambda width: pl.BlockSpec((None, tm, width), lambda b, i: (b, i, 0))
    prev = pl.BlockSpec((None, POOL_HALO, POOL_WIDTH),
                        lambda b, i: (b, jnp.maximum(i * (tm // POOL_HALO) - 1, 0), 0))
    return pl.pallas_call(
        functools.partial(_post_kernel, tm=tm, sub=POST_SUB_TOKENS, final=final),
        grid=(B, S // tm),
        in_specs=[tok(D),
                  pl.BlockSpec((None, HEADS_PER_GROUP, tm, HEAD_DIM), lambda b, i: (b, 0, i, 0)),
                  tok(POOL_WIDTH), prev, tok(D), tok(D),
                  pl.BlockSpec((None, None, tm, p.shape[-1]), lambda b, i: (layer, b, i, 0))]
        + [_layer_block(w, layer) for w in weights]
        + [pl.BlockSpec(g_final.shape, lambda b, i: (0, 0), pipeline_mode=pl.Buffered(1))],
        out_specs=tok(D),
        out_shape=jax.ShapeDtypeStruct((B, S, D), F32),
        scratch_shapes=[pltpu.VMEM((tm, D), BF16),
                        pltpu.VMEM((2, POOL_WIDTH // LANES, POOL_HALO + POST_SUB_TOKENS, LANES), F32),
                        pltpu.VMEM((CONV_SLABS, POST_SUB_TOKENS + CONV_HALO, LANES), F32),
                        pltpu.VMEM((CONV_HALO, d_up), F32),
                        pltpu.VMEM((tm, d_up // 2), BF16)],
        compiler_params=_PARAMS,
        name="post",
    )(x, attn, pool, pool, ga, gb, p, *weights, g_final)


def _rotary_tables(seq):
    pos = jnp.arange(seq, dtype=F32)
    inv_freq = jnp.exp(jnp.arange(0, ROPE_DIM, 2, dtype=F32) * (-math.log(ROPE_THETA) / ROPE_DIM))
    ang = pos[:, None] * inv_freq[None, :]
    cos, sin = jnp.cos(ang), jnp.sin(ang)
    rest = HEAD_DIM - ROPE_DIM
    c = jnp.concatenate([cos, cos, jnp.ones((seq, rest), F32)], axis=1)
    s1 = jnp.concatenate([-sin, jnp.zeros((seq, HEAD_DIM - ROPE_HALF), F32)], axis=1)
    s2 = jnp.concatenate([jnp.zeros((seq, ROPE_HALF), F32), sin, jnp.zeros((seq, rest), F32)], axis=1)
    scale = HEAD_DIM ** -0.5
    return (c * scale, s1 * scale, s2 * scale, c, s1, s2)


def kernel(x, p, g_mix, w_in, w_ya, w_yb, pool_w, pool_scale, w_o, g_ffn, w_up, conv_w, conv_b,
           w_down, g_ple, w_ple, w_ple_gate, g_final):
    B, S, D = x.shape
    depth = w_in.shape[0]
    for window, d in ATTN_PATTERNS:
        assert window // d == BLOCK and S % (BLOCK * d) == 0
    assert S % INPROJ_TOKENS == 0
    tables = _rotary_tables(S)
    rows = lambda a: a.reshape(depth, 1, a.shape[-1])
    g_mix = rows(g_mix)
    w_in = w_in.astype(BF16)
    post_weights = (w_ya.astype(BF16), pool_w.astype(BF16), rows(pool_scale), w_yb.astype(BF16),
                    w_o.astype(BF16), rows(g_ffn), w_up.astype(BF16), conv_w, rows(conv_b),
                    w_down.astype(BF16), rows(g_ple), w_ple.astype(BF16), w_ple_gate.astype(BF16))
    g_final = g_final.reshape(1, D)
    for layer in range(depth):
        z = _inproj(x, g_mix, w_in, tables, layer, INPROJ_TOKENS)
        attn = _attention(z[:9], S)
        pool, ga, gb = z[9:]
        x = _post(x, attn, pool, ga, gb, p, post_weights, g_final, layer, POST_TOKENS,
                  final=(layer == depth - 1))
    return x
```

```python
import functools
import math

import jax
import jax.numpy as jnp
from jax import lax
from jax.experimental import pallas as pl
from jax.experimental.pallas import tpu as pltpu

F32 = jnp.float32
BF16 = jnp.bfloat16

HEAD_DIM = 128
HEADS_PER_GROUP = 4
GROUP_WIDTH = HEADS_PER_GROUP * HEAD_DIM
ATTN_PATTERNS = ((128, 1), (512, 4), (2048, 16))
ATTN_WIDTH = len(ATTN_PATTERNS) * GROUP_WIDTH
ROPE_DIM = HEAD_DIM // 4
ROPE_HALF = ROPE_DIM // 2
ROPE_THETA = 500000.0
BLOCK = 128
NEG_INF = -1e30
POOL_WINDOWS = (2, 4, 8, 16)
POOL_GROUP_WIDTH = 256
POOL_WIDTH = len(POOL_WINDOWS) * POOL_GROUP_WIDTH
POOL_HALO = 16
CONV_WIDTH = 3
CONV_HALO = 8
RMS_EPS = 1e-6

VMEM_LIMIT_BYTES = 58 * 1024 * 1024
QKV_TOKENS = 1024
POST_TOKENS = 256
FF_CHUNK = 256
PERM_SLABS = 8

_PARAMS = pltpu.CompilerParams(
    dimension_semantics=("arbitrary", "arbitrary"), vmem_limit_bytes=VMEM_LIMIT_BYTES)


def _rms_scale(x, g):
    ms = jnp.mean(x * x, axis=-1, keepdims=True)
    return x * lax.rsqrt(ms + RMS_EPS) * g


def _layer_block(arr, layer):
    tail = (0,) * (arr.ndim - 1)
    return pl.BlockSpec((None,) + arr.shape[1:], lambda *_: (layer,) + tail,
                        pipeline_mode=pl.Buffered(1))


def _dot(a, b):
    return jnp.dot(a, b, preferred_element_type=F32)


def _qkv_patterns():
    return [pat for pat in ATTN_PATTERNS for _ in range(3)]


def _qkv_kernel(x_ref, g_ref, w_ref, c_ref, s1_ref, s2_ref, *rest, tm):
    qkv_refs, (h_scr, perm_scr) = rest[:-2], rest[-2:]
    h_scr[...] = _rms_scale(x_ref[...], g_ref[...]).astype(BF16)

    def proj(col0):
        return _dot(h_scr[...], w_ref[:, col0:col0 + GROUP_WIDTH])

    def rotary(t, scale):
        c, s1, s2 = c_ref[...] * scale, s1_ref[...] * scale, s2_ref[...] * scale
        heads = []
        for hh in range(HEADS_PER_GROUP):
            th = t[:, hh * HEAD_DIM:(hh + 1) * HEAD_DIM]
            heads.append(th * c + pltpu.roll(th, HEAD_DIM - ROPE_HALF, 1) * s1
                         + pltpu.roll(th, ROPE_HALF, 1) * s2)
        return jnp.concatenate(heads, axis=1)

    ring = [0]

    def put(dst_ref, val, d):
        for hh in range(HEADS_PER_GROUP):
            head = val[:, hh * HEAD_DIM:(hh + 1) * HEAD_DIM]
            if d == 1:
                dst_ref[hh] = head.astype(BF16)
                continue
            scr = perm_scr.at[ring[0] % perm_scr.shape[0]]
            ring[0] += 1
            scr[...] = head
            for r in range(d):
                dst_ref[hh, r] = scr[pl.ds(r, tm // d, stride=d), :].astype(BF16)

    for g, (_, d) in reversed(list(enumerate(ATTN_PATTERNS))):
        q_ref, k_ref, v_ref = qkv_refs[3 * g:3 * g + 3]
        put(q_ref, rotary(proj(g * GROUP_WIDTH), HEAD_DIM ** -0.5), d)
        put(k_ref, rotary(proj(ATTN_WIDTH + g * GROUP_WIDTH), 1.0), d)
        put(v_ref, proj(2 * ATTN_WIDTH + g * GROUP_WIDTH), d)


def _qkv(x, g_mix, w_qkv, tables, layer, tm):
    B, S, D = x.shape
    tab = pl.BlockSpec((tm, HEAD_DIM), lambda b, i: (i, 0))
    out_shape, out_specs = [], []
    for _, d in _qkv_patterns():
        if d == 1:
            out_shape.append(jax.ShapeDtypeStruct((B, HEADS_PER_GROUP, S, HEAD_DIM), BF16))
            out_specs.append(pl.BlockSpec((None, HEADS_PER_GROUP, tm, HEAD_DIM),
                                          lambda b, i: (b, 0, i, 0)))
        else:
            out_shape.append(jax.ShapeDtypeStruct((B, HEADS_PER_GROUP, d, S // d, HEAD_DIM), BF16))
            out_specs.append(pl.BlockSpec((None, HEADS_PER_GROUP, d, tm // d, HEAD_DIM),
                                          lambda b, i: (b, 0, 0, i, 0)))
    return pl.pallas_call(
        functools.partial(_qkv_kernel, tm=tm),
        grid=(B, S // tm),
        in_specs=[pl.BlockSpec((None, tm, D), lambda b, i: (b, i, 0)),
                  _layer_block(g_mix, layer), _layer_block(w_qkv, layer), tab, tab, tab],
        out_specs=out_specs,
        out_shape=out_shape,
        scratch_shapes=[pltpu.VMEM((tm, D), BF16), pltpu.VMEM((PERM_SLABS, tm, HEAD_DIM), F32)],
        compiler_params=_PARAMS,
        name="qkv",
    )(x, g_mix, w_qkv, *tables)


def _attn_kernel(*args, seq):
    n_groups = len(ATTN_PATTERNS)
    refs = [args[3 * g:3 * g + 3] for g in range(n_groups)]
    o_ref, onorm_scr, lse_scr = args[3 * n_groups:]
    row = lax.broadcasted_iota(jnp.int32, (BLOCK, 2 * BLOCK), 0)
    col = lax.broadcasted_iota(jnp.int32, (BLOCK, 2 * BLOCK), 1)
    bias_two = jnp.where(col < BLOCK, jnp.where(col >= row, 0.0, NEG_INF),
                         jnp.where(col - BLOCK <= row, 0.0, NEG_INF)).astype(F32)
    row1 = lax.broadcasted_iota(jnp.int32, (BLOCK, BLOCK), 0)
    col1 = lax.broadcasted_iota(jnp.int32, (BLOCK, BLOCK), 1)
    bias_one = jnp.where(col1 <= row1, 0.0, NEG_INF).astype(F32)

    def block(ld, n):
        qr, kr, vr = ld
        lo, size, bias = (0, BLOCK, bias_one) if n == 0 else ((n - 1) * BLOCK, 2 * BLOCK, bias_two)
        s = lax.dot_general(qr(n * BLOCK, BLOCK), kr(lo, size), (((1,), (1,)), ((), ())),
                            preferred_element_type=F32) + bias
        m = jnp.max(s, axis=-1, keepdims=True)
        p = jnp.exp(s - m).astype(BF16)
        vext = jnp.concatenate([vr(lo, size), jnp.ones((size, HEAD_DIM), BF16)], axis=1)
        o = _dot(p, vext)
        l = o[:, HEAD_DIM:]
        return o[:, :HEAD_DIM] / l, jnp.broadcast_to(m, (BLOCK, HEAD_DIM)) + jnp.log(l)

    for g, (_, d) in enumerate(ATTN_PATTERNS):
        if d == 1:
            continue
        for r in range(d):
            ld = [lambda lo, size, ref=ref: ref[r, lo:lo + size, :] for ref in refs[g]]
            for n in range(seq // d // BLOCK):
                o, lse = block(ld, n)
                rows = pl.ds(n * BLOCK * d + r, BLOCK, stride=d)
                onorm_scr[g - 1, rows, :] = o
                lse_scr[g - 1, rows, :] = lse

    ld = [lambda lo, size, ref=ref: ref[lo:lo + size, :] for ref in refs[0]]
    for n in range(seq // BLOCK):
        rows = pl.ds(n * BLOCK, BLOCK)
        o, lse = block(ld, n)
        os_ = [o] + [onorm_scr[g, rows, :] for g in range(n_groups - 1)]
        lses = [lse] + [lse_scr[g, rows, :] for g in range(n_groups - 1)]
        mx = functools.reduce(jnp.maximum, lses)
        ws = [jnp.exp(li - mx) for li in lses]
        num = sum(w * oi for w, oi in zip(ws, os_))
        o_ref[rows, :] = (num / sum(ws)).astype(BF16)


def _attention(qkv, seq):
    B = qkv[0].shape[0]
    in_specs = []
    for _, d in _qkv_patterns():
        if d == 1:
            in_specs.append(pl.BlockSpec((None, None, seq, HEAD_DIM), lambda b, h: (b, h, 0, 0)))
        else:
            in_specs.append(pl.BlockSpec((None, None, d, seq // d, HEAD_DIM),
                                         lambda b, h: (b, h, 0, 0, 0)))
    n_dilated = len(ATTN_PATTERNS) - 1
    return pl.pallas_call(
        functools.partial(_attn_kernel, seq=seq),
        grid=(B, HEADS_PER_GROUP),
        in_specs=in_specs,
        out_specs=pl.BlockSpec((None, None, seq, HEAD_DIM), lambda b, h: (b, h, 0, 0)),
        out_shape=jax.ShapeDtypeStruct((B, HEADS_PER_GROUP, seq, HEAD_DIM), BF16),
        scratch_shapes=[pltpu.VMEM((n_dilated, seq, HEAD_DIM), F32)] * 2,
        compiler_params=_PARAMS,
        name="attention",
    )(*qkv)


def _post_kernel(x_ref, attn_ref, p_ref, gmix_ref, wside_ref,
                 wya_ref, poolw_ref, pscale_ref, wyb_ref, wo_ref, gffn_ref, wup_ref, cw_ref, cb_ref,
                 wdown_ref, gple_ref, wple_ref, wpleg_ref, gfin_ref, out_ref,
                 pool_halo, u_scr, *, tm, final):
    i = pl.program_id(1)
    d_ff = wdown_ref.shape[0]
    d_model = x_ref.shape[-1]
    x = x_ref[...]

    @pl.when(i == 0)
    def _():
        pool_halo[...] = jnp.zeros(pool_halo.shape, F32)
        u_scr[0:CONV_HALO, :] = jnp.zeros((CONV_HALO, u_scr.shape[1]), F32)

    side = _dot(_rms_scale(x, gmix_ref[...]).astype(BF16), wside_ref[...])
    u = side[:, :POOL_WIDTH]
    gate_a = side[:, POOL_WIDTH:POOL_WIDTH + d_model]
    gate_b = side[:, POOL_WIDTH + d_model:]

    attn = jnp.concatenate([attn_ref[hh] for hh in range(HEADS_PER_GROUP)], axis=1)
    y_a = _dot(attn, wya_ref[...])

    prev = pool_halo[...]
    pool_halo[...] = u[tm - POOL_HALO:, :]
    tok = i * tm + lax.broadcasted_iota(jnp.int32, (tm, 1), 0)
    mixed = []
    for g, w in enumerate(POOL_WINDOWS):
        cols = slice(g * POOL_GROUP_WIDTH, (g + 1) * POOL_GROUP_WIDTH)
        ug = u[:, cols]
        s = jnp.concatenate([prev[:, cols], ug], axis=0)
        shift = 1
        while shift < w:
            s = s + pltpu.roll(s, shift, 0)
            shift *= 2
        count = jnp.minimum(tok + 1, w).astype(F32)
        pooled = s[POOL_HALO:] / count - ug
        mixed.append(_dot(pooled.astype(BF16), poolw_ref[g]))
    mixed = jnp.concatenate(mixed, axis=1) * pscale_ref[...]
    y_b = _dot(mixed.astype(BF16), wyb_ref[...])

    merged = jax.nn.sigmoid(gate_a) * y_a + jax.nn.sigmoid(gate_b) * y_b
    x = x + _dot(merged.astype(BF16), wo_ref[...])

    h = _rms_scale(x, gffn_ref[...]).astype(BF16)
    u_scr[CONV_HALO:CONV_HALO + tm, :] = _dot(h, wup_ref[...])

    def conv(cols):
        y = cb_ref[:, cols]
        for tap in range(CONV_WIDTH):
            lo = CONV_HALO - (CONV_WIDTH - 1 - tap)
            y = y + cw_ref[tap:tap + 1, cols] * u_scr[lo:lo + tm, cols]
        return y

    acts = []
    for j in range(d_ff // FF_CHUNK):
        gate = conv(slice(j * FF_CHUNK, (j + 1) * FF_CHUNK))
        val = conv(slice(d_ff + j * FF_CHUNK, d_ff + (j + 1) * FF_CHUNK))
        acts.append((gate * jax.nn.sigmoid(gate) * val).astype(BF16))
    u_scr[0:CONV_HALO, :] = u_scr[tm:tm + CONV_HALO, :]
    x = x + _dot(jnp.concatenate(acts, axis=1), wdown_ref[...])

    h = _rms_scale(x, gple_ref[...]).astype(BF16)
    gate = jax.nn.sigmoid(_dot(h, wpleg_ref[...]))
    x = x + _dot(p_ref[...].astype(BF16), wple_ref[...]) * gate
    if final:
        x = _rms_scale(x, gfin_ref[...])
    out_ref[...] = x


def _post(x, attn, p, weights, g_final, layer, tm, final):
    B, S, D = x.shape
    assert S % tm == 0 and tm >= POOL_HALO
    d_up = weights[8].shape[-1]
    tok = pl.BlockSpec((None, tm, D), lambda b, i: (b, i, 0))
    return pl.pallas_call(
        functools.partial(_post_kernel, tm=tm, final=final),
        grid=(B, S // tm),
        in_specs=[tok,
                  pl.BlockSpec((None, HEADS_PER_GROUP, tm, HEAD_DIM), lambda b, i: (b, 0, i, 0)),
                  pl.BlockSpec((None, None, tm, p.shape[-1]), lambda b, i: (layer, b, i, 0))]
        + [_layer_block(w, layer) for w in weights]
        + [pl.BlockSpec(g_final.shape, lambda b, i: (0, 0), pipeline_mode=pl.Buffered(1))],
        out_specs=tok,
        out_shape=jax.ShapeDtypeStruct((B, S, D), F32),
        scratch_shapes=[pltpu.VMEM((POOL_HALO, POOL_WIDTH), F32),
                        pltpu.VMEM((tm + CONV_HALO, d_up), F32)],
        compiler_params=_PARAMS,
        name="post",
    )(x, attn, p, *weights, g_final)


def _rotary_tables(seq):
    pos = jnp.arange(seq, dtype=F32)
    inv_freq = jnp.exp(jnp.arange(0, ROPE_DIM, 2, dtype=F32) * (-math.log(ROPE_THETA) / ROPE_DIM))
    ang = pos[:, None] * inv_freq[None, :]
    cos, sin = jnp.cos(ang), jnp.sin(ang)
    rest = HEAD_DIM - ROPE_DIM
    c = jnp.concatenate([cos, cos, jnp.ones((seq, rest), F32)], axis=1)
    s1 = jnp.concatenate([-sin, jnp.zeros((seq, HEAD_DIM - ROPE_HALF), F32)], axis=1)
    s2 = jnp.concatenate([jnp.zeros((seq, ROPE_HALF), F32), sin, jnp.zeros((seq, rest), F32)], axis=1)
    return (c, s1, s2)


def kernel(x, p, g_mix, w_in, w_ya, w_yb, pool_w, pool_scale, w_o, g_ffn, w_up, conv_w, conv_b,
           w_down, g_ple, w_ple, w_ple_gate, g_final):
    B, S, D = x.shape
    depth = w_in.shape[0]
    assert S % QKV_TOKENS == 0
    for window, d in ATTN_PATTERNS:
        assert window // d == BLOCK and S % (BLOCK * d) == 0
        assert QKV_TOKENS % (16 * d) == 0
    assert ATTN_PATTERNS[0][1] == 1 and all(d > 1 for _, d in ATTN_PATTERNS[1:])
    tables = _rotary_tables(S)
    rows = lambda a: a.reshape(depth, 1, a.shape[-1])
    g_mix = rows(g_mix)
    w_qkv = w_in[:, :, :3 * ATTN_WIDTH].astype(BF16)
    w_side = w_in[:, :, 3 * ATTN_WIDTH:].astype(BF16)
    post_weights = (g_mix, w_side,
                    w_ya.astype(BF16), pool_w.astype(BF16), rows(pool_scale), w_yb.astype(BF16),
                    w_o.astype(BF16), rows(g_ffn), w_up.astype(BF16), conv_w, rows(conv_b),
                    w_down.astype(BF16), rows(g_ple), w_ple.astype(BF16), w_ple_gate.astype(BF16))
    g_final = g_final.reshape(1, D)
    for layer in range(depth):
        qkv = _qkv(x, g_mix, w_qkv, tables, layer, QKV_TOKENS)
        attn = _attention(qkv, S)
        x = _post(x, attn, p, post_weights, g_final, layer, POST_TOKENS, final=(layer == depth - 1))
    return x
```

```python
import functools
import math

import jax
import jax.numpy as jnp
from jax import lax
from jax.experimental import pallas as pl
from jax.experimental.pallas import tpu as pltpu

F32 = jnp.float32
BF16 = jnp.bfloat16

HEAD_DIM = 128
HEADS_PER_GROUP = 4
GROUP_WIDTH = HEADS_PER_GROUP * HEAD_DIM
ATTN_PATTERNS = ((128, 1), (512, 4), (2048, 16))
ATTN_WIDTH = len(ATTN_PATTERNS) * GROUP_WIDTH
ROPE_DIM = HEAD_DIM // 4
ROPE_HALF = ROPE_DIM // 2
ROPE_THETA = 500000.0
ROPE_PARTNER = HEAD_DIM // 2
BLOCK = 128
NEG_INF = -1e30
POOL_WINDOWS = (2, 4, 8, 16)
POOL_GROUP_WIDTH = 256
POOL_WIDTH = len(POOL_WINDOWS) * POOL_GROUP_WIDTH
POOL_HALO = 16
CONV_WIDTH = 3
CONV_HALO = 8
RMS_EPS = 1e-6
Q_SCALE = HEAD_DIM ** -0.5 * math.log2(math.e)

VMEM_LIMIT_BYTES = 58 * 1024 * 1024
QKV_TOKENS = 1024
POST_TOKENS = 256
FF_CHUNK = 256
ATTN_HEADS_PER_STEP = 2
PERM_SLABS = 8

_PARAMS = pltpu.CompilerParams(
    dimension_semantics=("arbitrary", "arbitrary"), vmem_limit_bytes=VMEM_LIMIT_BYTES)


def _rms_scale(x, g):
    ms = jnp.mean(x * x, axis=-1, keepdims=True)
    return x * lax.rsqrt(ms + RMS_EPS) * g


def _layer_block(arr, layer):
    tail = (0,) * (arr.ndim - 1)
    return pl.BlockSpec((None,) + arr.shape[1:], lambda *_: (layer,) + tail,
                        pipeline_mode=pl.Buffered(1))


def _dot(a, b):
    return jnp.dot(a, b, preferred_element_type=F32)


def _qkv_patterns():
    return [pat for pat in ATTN_PATTERNS for _ in range(3)]


def _qkv_kernel(x_ref, g_ref, w_ref, c_ref, s_ref, *rest, tm):
    qkv_refs, perm_scr = rest[:-1], rest[-1]
    h = _rms_scale(x_ref[...], g_ref[...]).astype(BF16)

    def rotary(t, scale):
        c, s = c_ref[...] * scale, s_ref[...] * scale
        heads = []
        for hh in range(HEADS_PER_GROUP):
            th = t[:, hh * HEAD_DIM:(hh + 1) * HEAD_DIM]
            heads.append(th * c + pltpu.roll(th, ROPE_PARTNER, 1) * s)
        return jnp.concatenate(heads, axis=1)

    ring = [0]

    def put(dst_ref, val, d):
        for hh in range(HEADS_PER_GROUP):
            head = val[:, hh * HEAD_DIM:(hh + 1) * HEAD_DIM]
            if d == 1:
                dst_ref[hh] = head.astype(BF16)
                continue
            scr = perm_scr.at[ring[0] % perm_scr.shape[0]]
            ring[0] += 1
            scr[...] = head
            for r in range(d):
                dst_ref[hh, r] = scr[pl.ds(r, tm // d, stride=d), :].astype(BF16)

    for g, (_, d) in reversed(list(enumerate(ATTN_PATTERNS))):
        q_ref, k_ref, v_ref = qkv_refs[3 * g:3 * g + 3]
        col = lambda j: slice(j * ATTN_WIDTH + g * GROUP_WIDTH, j * ATTN_WIDTH + (g + 1) * GROUP_WIDTH)
        put(q_ref, rotary(_dot(h, w_ref[:, col(0)]), Q_SCALE), d)
        put(k_ref, rotary(_dot(h, w_ref[:, col(1)]), 1.0), d)
        put(v_ref, _dot(h, w_ref[:, col(2)]), d)


def _qkv(x, g_mix, w_qkv, tables, layer, tm):
    B, S, D = x.shape
    tab = pl.BlockSpec((tm, HEAD_DIM), lambda b, i: (i, 0))
    out_shape, out_specs = [], []
    for _, d in _qkv_patterns():
        if d == 1:
            out_shape.append(jax.ShapeDtypeStruct((B, HEADS_PER_GROUP, S, HEAD_DIM), BF16))
            out_specs.append(pl.BlockSpec((None, HEADS_PER_GROUP, tm, HEAD_DIM),
                                          lambda b, i: (b, 0, i, 0)))
        else:
            out_shape.append(jax.ShapeDtypeStruct((B, HEADS_PER_GROUP, d, S // d, HEAD_DIM), BF16))
            out_specs.append(pl.BlockSpec((None, HEADS_PER_GROUP, d, tm // d, HEAD_DIM),
                                          lambda b, i: (b, 0, 0, i, 0)))
    return pl.pallas_call(
        functools.partial(_qkv_kernel, tm=tm),
        grid=(B, S // tm),
        in_specs=[pl.BlockSpec((None, tm, D), lambda b, i: (b, i, 0)),
                  _layer_block(g_mix, layer), _layer_block(w_qkv, layer), tab, tab],
        out_specs=out_specs,
        out_shape=out_shape,
        scratch_shapes=[pltpu.VMEM((PERM_SLABS, tm, HEAD_DIM), F32)],
        compiler_params=_PARAMS,
        name="qkv",
    )(x, g_mix, w_qkv, *tables)


def _attn_kernel(*args, seq):
    n_groups = len(ATTN_PATTERNS)
    refs = [args[3 * g:3 * g + 3] for g in range(n_groups)]
    o_ref, acc_scr, l_scr, m_scr = args[3 * n_groups:]
    row = lax.broadcasted_iota(jnp.int32, (BLOCK, 2 * BLOCK), 0)
    col = lax.broadcasted_iota(jnp.int32, (BLOCK, 2 * BLOCK), 1)
    bias_two = jnp.where(col < BLOCK, jnp.where(col >= row, 0.0, NEG_INF),
                         jnp.where(col - BLOCK <= row, 0.0, NEG_INF)).astype(F32)
    row1 = lax.broadcasted_iota(jnp.int32, (BLOCK, BLOCK), 0)
    col1 = lax.broadcasted_iota(jnp.int32, (BLOCK, BLOCK), 1)
    bias_one = jnp.where(col1 <= row1, 0.0, NEG_INF).astype(F32)

    def subsequence(heads, n_blocks):
        for n in range(n_blocks):
            for q_at, k_at, v_at, store in heads:
                lo, bias = (0, bias_one) if n == 0 else (n - 1, bias_two)
                kk, vv = k_at(lo, n), v_at(lo, n)
                s = lax.dot_general(q_at(n), kk, (((1,), (1,)), ((), ())),
                                    preferred_element_type=F32) + bias
                m = jnp.max(s, axis=-1, keepdims=True)
                p = jnp.exp2(s - m).astype(BF16)
                vext = jnp.concatenate([vv, jnp.ones((vv.shape[0], HEAD_DIM), BF16)], axis=1)
                o = _dot(p, vext)
                store(n, o[:, :HEAD_DIM], o[:, HEAD_DIM:], jnp.broadcast_to(m, (BLOCK, HEAD_DIM)))

    n_heads = o_ref.shape[0]
    blk = lambda lo, hi: slice(lo * BLOCK, (hi + 1) * BLOCK)
    for g, (_, d) in enumerate(ATTN_PATTERNS):
        if d == 1:
            continue
        q_ref, k_ref, v_ref = refs[g]
        for r in range(d):
            def head(hd, g=g, d=d, r=r, q_ref=q_ref, k_ref=k_ref, v_ref=v_ref):
                def store(n, acc, l, m):
                    rows = pl.ds(n * BLOCK * d + r, BLOCK, stride=d)
                    acc_scr[hd, g - 1, rows, :] = acc
                    l_scr[hd, g - 1, rows, :] = l
                    m_scr[hd, g - 1, rows, :] = m
                return (lambda n: q_ref[hd, r, blk(n, n), :], lambda lo, hi: k_ref[hd, r, blk(lo, hi), :],
                        lambda lo, hi: v_ref[hd, r, blk(lo, hi), :], store)
            subsequence([head(hd) for hd in range(n_heads)], seq // d // BLOCK)

    q_ref, k_ref, v_ref = refs[0]

    def head(hd):
        def merge(n, acc, l, m):
            rows = pl.ds(n * BLOCK, BLOCK)
            accs = [acc] + [acc_scr[hd, g, rows, :] for g in range(n_groups - 1)]
            ls = [l] + [l_scr[hd, g, rows, :] for g in range(n_groups - 1)]
            ms = [m] + [m_scr[hd, g, rows, :] for g in range(n_groups - 1)]
            mx = functools.reduce(jnp.maximum, ms)
            ws = [jnp.exp2(mi - mx) for mi in ms]
            num = sum(w * a for w, a in zip(ws, accs))
            den = sum(w * li for w, li in zip(ws, ls))
            o_ref[hd, rows, :] = (num / den).astype(BF16)
        return (lambda n: q_ref[hd, blk(n, n), :], lambda lo, hi: k_ref[hd, blk(lo, hi), :],
                lambda lo, hi: v_ref[hd, blk(lo, hi), :], merge)
    subsequence([head(hd) for hd in range(n_heads)], seq // BLOCK)


def _attention(qkv, seq):
    B = qkv[0].shape[0]
    nh = ATTN_HEADS_PER_STEP
    in_specs = []
    for _, d in _qkv_patterns():
        if d == 1:
            in_specs.append(pl.BlockSpec((None, nh, seq, HEAD_DIM), lambda b, h: (b, h, 0, 0)))
        else:
            in_specs.append(pl.BlockSpec((None, nh, d, seq // d, HEAD_DIM),
                                         lambda b, h: (b, h, 0, 0, 0)))
    n_dilated = len(ATTN_PATTERNS) - 1
    return pl.pallas_call(
        functools.partial(_attn_kernel, seq=seq),
        grid=(B, HEADS_PER_GROUP // nh),
        in_specs=in_specs,
        out_specs=pl.BlockSpec((None, nh, seq, HEAD_DIM), lambda b, h: (b, h, 0, 0)),
        out_shape=jax.ShapeDtypeStruct((B, HEADS_PER_GROUP, seq, HEAD_DIM), BF16),
        scratch_shapes=[pltpu.VMEM((nh, n_dilated, seq, HEAD_DIM), F32)] * 3,
        compiler_params=_PARAMS,
        name="attention",
    )(*qkv)


def _post_kernel(x_ref, attn_ref, p_ref, gmix_ref, wside_ref,
                 wya_ref, poolw_ref, pscale_ref, wyb_ref, wo_ref, gffn_ref, wup_ref, cw_ref, cb_ref,
                 wdown_ref, gple_ref, wple_ref, wpleg_ref, gfin_ref, out_ref,
                 pool_halo, u_scr, *, tm, final):
    i = pl.program_id(1)
    d_ff = wdown_ref.shape[0]
    d_model = x_ref.shape[-1]
    x = x_ref[...]

    @pl.when(i == 0)
    def _():
        pool_halo[...] = jnp.zeros(pool_halo.shape, F32)
        u_scr[0:CONV_HALO, :] = jnp.zeros((CONV_HALO, u_scr.shape[1]), F32)

    attn = jnp.concatenate([attn_ref[hh] for hh in range(HEADS_PER_GROUP)], axis=1)
    y_a = _dot(attn, wya_ref[...])

    side = _dot(_rms_scale(x, gmix_ref[...]).astype(BF16), wside_ref[...])
    u = side[:, :POOL_WIDTH]
    gate_a = side[:, POOL_WIDTH:POOL_WIDTH + d_model]
    gate_b = side[:, POOL_WIDTH + d_model:]

    prev = pool_halo[...]
    pool_halo[...] = u[tm - POOL_HALO:, :]
    tok = i * tm + lax.broadcasted_iota(jnp.int32, (tm, 1), 0)
    mixed = []
    for g, w in enumerate(POOL_WINDOWS):
        cols = slice(g * POOL_GROUP_WIDTH, (g + 1) * POOL_GROUP_WIDTH)
        ug = u[:, cols]
        s = jnp.concatenate([prev[:, cols], ug], axis=0)
        shift = 1
        while shift < w:
            s = s + pltpu.roll(s, shift, 0)
            shift *= 2
        count = jnp.minimum(tok + 1, w).astype(F32)
        pooled = s[POOL_HALO:] / count - ug
        mixed.append(_dot(pooled.astype(BF16), poolw_ref[g]))
    mixed = jnp.concatenate(mixed, axis=1) * pscale_ref[...]
    y_b = _dot(mixed.astype(BF16), wyb_ref[...])

    merged = jax.nn.sigmoid(gate_a) * y_a + jax.nn.sigmoid(gate_b) * y_b
    x = x + _dot(merged.astype(BF16), wo_ref[...])

    h = _rms_scale(x, gffn_ref[...]).astype(BF16)
    u_scr[CONV_HALO:CONV_HALO + tm, :] = _dot(h, wup_ref[...])

    def conv(cols):
        y = cb_ref[:, cols]
        for tap in range(CONV_WIDTH):
            lo = CONV_HALO - (CONV_WIDTH - 1 - tap)
            y = y + cw_ref[tap:tap + 1, cols] * u_scr[lo:lo + tm, cols]
        return y

    acts = []
    for j in range(d_ff // FF_CHUNK):
        gate = conv(slice(j * FF_CHUNK, (j + 1) * FF_CHUNK))
        val = conv(slice(d_ff + j * FF_CHUNK, d_ff + (j + 1) * FF_CHUNK))
        acts.append((gate * jax.nn.sigmoid(gate) * val).astype(BF16))
    u_scr[0:CONV_HALO, :] = u_scr[tm:tm + CONV_HALO, :]
    x = x + _dot(jnp.concatenate(acts, axis=1), wdown_ref[...])

    h = _rms_scale(x, gple_ref[...]).astype(BF16)
    gate = jax.nn.sigmoid(_dot(h, wpleg_ref[...]))
    x = x + _dot(p_ref[...].astype(BF16), wple_ref[...]) * gate
    if final:
        x = _rms_scale(x, gfin_ref[...])
    out_ref[...] = x


def _post(x, attn, p, weights, g_final, layer, tm, final):
    B, S, D = x.shape
    assert S % tm == 0 and tm >= POOL_HALO
    d_up = weights[8].shape[-1]
    tok = pl.BlockSpec((None, tm, D), lambda b, i: (b, i, 0))
    return pl.pallas_call(
        functools.partial(_post_kernel, tm=tm, final=final),
        grid=(B, S // tm),
        in_specs=[tok,
                  pl.BlockSpec((None, HEADS_PER_GROUP, tm, HEAD_DIM), lambda b, i: (b, 0, i, 0)),
                  pl.BlockSpec((None, None, tm, p.shape[-1]), lambda b, i: (layer, b, i, 0))]
        + [_layer_block(w, layer) for w in weights]
        + [pl.BlockSpec(g_final.shape, lambda b, i: (0, 0), pipeline_mode=pl.Buffered(1))],
        out_specs=tok,
        out_shape=jax.ShapeDtypeStruct((B, S, D), F32),
        scratch_shapes=[pltpu.VMEM((POOL_HALO, POOL_WIDTH), F32),
                        pltpu.VMEM((tm + CONV_HALO, d_up), F32)],
        compiler_params=_PARAMS,
        name="post",
    )(x, attn, p, *weights, g_final)


def _rope_column_order():
    order = list(range(HEAD_DIM))
    for i in range(ROPE_HALF):
        a, b = ROPE_HALF + i, ROPE_PARTNER + i
        order[a], order[b] = order[b], order[a]
    return order


def _rotary_tables(seq):
    pos = jnp.arange(seq, dtype=F32)
    inv_freq = jnp.exp(jnp.arange(0, ROPE_DIM, 2, dtype=F32) * (-math.log(ROPE_THETA) / ROPE_DIM))
    ang = pos[:, None] * inv_freq[None, :]
    cos, sin = jnp.cos(ang), jnp.sin(ang)
    gap = jnp.zeros((seq, ROPE_PARTNER - ROPE_HALF), F32)
    c = jnp.concatenate([cos, gap + 1.0, cos, gap + 1.0], axis=1)
    s = jnp.concatenate([-sin, gap, sin, gap], axis=1)
    return (c, s)


def kernel(x, p, g_mix, w_in, w_ya, w_yb, pool_w, pool_scale, w_o, g_ffn, w_up, conv_w, conv_b,
           w_down, g_ple, w_ple, w_ple_gate, g_final):
    B, S, D = x.shape
    depth = w_in.shape[0]
    assert S % QKV_TOKENS == 0 and HEADS_PER_GROUP % ATTN_HEADS_PER_STEP == 0
    for window, d in ATTN_PATTERNS:
        assert window // d == BLOCK and S % (BLOCK * d) == 0
        assert QKV_TOKENS % (16 * d) == 0
    assert ATTN_PATTERNS[0][1] == 1 and all(d > 1 for _, d in ATTN_PATTERNS[1:])
    tables = _rotary_tables(S)
    rows = lambda a: a.reshape(depth, 1, a.shape[-1])
    g_mix = rows(g_mix)
    qk = w_in[:, :, :2 * ATTN_WIDTH].reshape(depth, D, -1, HEAD_DIM)[..., jnp.array(_rope_column_order())]
    w_qkv = jnp.concatenate([qk.reshape(depth, D, 2 * ATTN_WIDTH),
                             w_in[:, :, 2 * ATTN_WIDTH:3 * ATTN_WIDTH]], axis=-1).astype(BF16)
    w_side = w_in[:, :, 3 * ATTN_WIDTH:].astype(BF16)
    post_weights = (g_mix, w_side,
                    w_ya.astype(BF16), pool_w.astype(BF16), rows(pool_scale), w_yb.astype(BF16),
                    w_o.astype(BF16), rows(g_ffn), w_up.astype(BF16), conv_w, rows(conv_b),
                    w_down.astype(BF16), rows(g_ple), w_ple.astype(BF16), w_ple_gate.astype(BF16))
    g_final = g_final.reshape(1, D)
    for layer in range(depth):
        qkv = _qkv(x, g_mix, w_qkv, tables, layer, QKV_TOKENS)
        attn = _attention(qkv, S)
        x = _post(x, attn, p, post_weights, g_final, layer, POST_TOKENS, final=(layer == depth - 1))
    return x
```

```python
import functools
import math

import jax
import jax.numpy as jnp
from jax import lax
from jax.experimental import pallas as pl
from jax.experimental.pallas import tpu as pltpu

F32 = jnp.float32
BF16 = jnp.bfloat16

HEAD_DIM = 128
HEADS_PER_GROUP = 4
GROUP_WIDTH = HEADS_PER_GROUP * HEAD_DIM
ATTN_PATTERNS = ((128, 1), (512, 4), (2048, 16))
ATTN_WIDTH = len(ATTN_PATTERNS) * GROUP_WIDTH
ROPE_DIM = HEAD_DIM // 4
ROPE_HALF = ROPE_DIM // 2
ROPE_THETA = 500000.0
ROPE_PARTNER = HEAD_DIM // 2
BLOCK = 128
NEG_INF = -1e30
POOL_WINDOWS = (2, 4, 8, 16)
POOL_GROUP_WIDTH = 256
POOL_WIDTH = len(POOL_WINDOWS) * POOL_GROUP_WIDTH
POOL_HALO = 16
CONV_WIDTH = 3
CONV_HALO = 8
RMS_EPS = 1e-6
Q_SCALE = HEAD_DIM ** -0.5 * math.log2(math.e)

VMEM_LIMIT_BYTES = 58 * 1024 * 1024
QKV_TOKENS = 1024
POST_TOKENS = 256
FF_CHUNK = 256
ATTN_HEADS_PER_STEP = 2
PERM_SLABS = 8

_PARAMS = pltpu.CompilerParams(
    dimension_semantics=("arbitrary", "arbitrary"), vmem_limit_bytes=VMEM_LIMIT_BYTES)


def _rms_scale(x, g):
    ms = jnp.mean(x * x, axis=-1, keepdims=True)
    return x * lax.rsqrt(ms + RMS_EPS) * g


def _layer_block(arr, layer):
    tail = (0,) * (arr.ndim - 1)
    return pl.BlockSpec((None,) + arr.shape[1:], lambda *_: (layer,) + tail,
                        pipeline_mode=pl.Buffered(1))


def _dot(a, b):
    return jnp.dot(a, b, preferred_element_type=F32)


def _qkv_patterns():
    return [pat for pat in ATTN_PATTERNS for _ in range(3)]


def _row_pitch(d):
    return d + 1 if d % 8 == 0 else d


def _pitched_rows(n, d):
    return n // d * _row_pitch(d)


def _qkv_kernel(x_ref, g_ref, w_ref, c_ref, s_ref, *rest, tm):
    qkv_refs, perm_scr = rest[:-1], rest[-1]
    h = _rms_scale(x_ref[...], g_ref[...]).astype(BF16)

    def rotary(t, scale):
        c, s = c_ref[...] * scale, s_ref[...] * scale
        heads = []
        for hh in range(HEADS_PER_GROUP):
            th = t[:, hh * HEAD_DIM:(hh + 1) * HEAD_DIM]
            heads.append(th * c + pltpu.roll(th, ROPE_PARTNER, 1) * s)
        return jnp.concatenate(heads, axis=1)

    ring = [0]

    def put(dst_ref, val, d):
        for hh in range(HEADS_PER_GROUP):
            head = val[:, hh * HEAD_DIM:(hh + 1) * HEAD_DIM]
            if d == 1:
                dst_ref[hh] = head.astype(BF16)
                continue
            scr = perm_scr.at[ring[0] % perm_scr.shape[0]]
            ring[0] += 1
            pitch = _row_pitch(d)
            if pitch == d:
                scr[0:tm, :] = head
            else:
                for j in range(tm // d):
                    scr[pl.ds(pitch * j, d), :] = head[d * j:d * (j + 1), :]
            for r in range(d):
                dst_ref[hh, r] = scr[pl.ds(r, tm // d, stride=pitch), :].astype(BF16)

    for g, (_, d) in reversed(list(enumerate(ATTN_PATTERNS))):
        q_ref, k_ref, v_ref = qkv_refs[3 * g:3 * g + 3]
        col = lambda j: slice(j * ATTN_WIDTH + g * GROUP_WIDTH, j * ATTN_WIDTH + (g + 1) * GROUP_WIDTH)
        put(q_ref, rotary(_dot(h, w_ref[:, col(0)]), Q_SCALE), d)
        put(k_ref, rotary(_dot(h, w_ref[:, col(1)]), 1.0), d)
        put(v_ref, _dot(h, w_ref[:, col(2)]), d)


def _qkv(x, g_mix, w_qkv, tables, layer, tm):
    B, S, D = x.shape
    tab = pl.BlockSpec((tm, HEAD_DIM), lambda b, i: (i, 0))
    out_shape, out_specs = [], []
    for _, d in _qkv_patterns():
        if d == 1:
            out_shape.append(jax.ShapeDtypeStruct((B, HEADS_PER_GROUP, S, HEAD_DIM), BF16))
            out_specs.append(pl.BlockSpec((None, HEADS_PER_GROUP, tm, HEAD_DIM),
                                          lambda b, i: (b, 0, i, 0)))
        else:
            out_shape.append(jax.ShapeDtypeStruct((B, HEADS_PER_GROUP, d, S // d, HEAD_DIM), BF16))
            out_specs.append(pl.BlockSpec((None, HEADS_PER_GROUP, d, tm // d, HEAD_DIM),
                                          lambda b, i: (b, 0, 0, i, 0)))
    return pl.pallas_call(
        functools.partial(_qkv_kernel, tm=tm),
        grid=(B, S // tm),
        in_specs=[pl.BlockSpec((None, tm, D), lambda b, i: (b, i, 0)),
                  _layer_block(g_mix, layer), _layer_block(w_qkv, layer), tab, tab],
        out_specs=out_specs,
        out_shape=out_shape,
        scratch_shapes=[pltpu.VMEM(
            (PERM_SLABS, max(_pitched_rows(tm, d) for _, d in ATTN_PATTERNS), HEAD_DIM), F32)],
        compiler_params=_PARAMS,
        name="qkv",
    )(x, g_mix, w_qkv, *tables)


def _attn_kernel(*args, seq):
    n_groups = len(ATTN_PATTERNS)
    refs = [args[3 * g:3 * g + 3] for g in range(n_groups)]
    o_ref, acc_scr, l_scr, m_scr = args[3 * n_groups:]
    row = lax.broadcasted_iota(jnp.int32, (BLOCK, 2 * BLOCK), 0)
    col = lax.broadcasted_iota(jnp.int32, (BLOCK, 2 * BLOCK), 1)
    bias_two = jnp.where(col < BLOCK, jnp.where(col >= row, 0.0, NEG_INF),
                         jnp.where(col - BLOCK <= row, 0.0, NEG_INF)).astype(F32)
    row1 = lax.broadcasted_iota(jnp.int32, (BLOCK, BLOCK), 0)
    col1 = lax.broadcasted_iota(jnp.int32, (BLOCK, BLOCK), 1)
    bias_one = jnp.where(col1 <= row1, 0.0, NEG_INF).astype(F32)

    def subsequence(heads, n_blocks):
        for n in range(n_blocks):
            for q_at, k_at, v_at, store in heads:
                lo, bias = (0, bias_one) if n == 0 else (n - 1, bias_two)
                kk, vv = k_at(lo, n), v_at(lo, n)
                s = lax.dot_general(q_at(n), kk, (((1,), (1,)), ((), ())),
                                    preferred_element_type=F32) + bias
                m = jnp.max(s, axis=-1, keepdims=True)
                p = jnp.exp2(s - m).astype(BF16)
                vext = jnp.concatenate([vv, jnp.ones((vv.shape[0], HEAD_DIM), BF16)], axis=1)
                o = _dot(p, vext)
                store(n, o[:, :HEAD_DIM], o[:, HEAD_DIM:], jnp.broadcast_to(m, (BLOCK, HEAD_DIM)))

    n_heads = o_ref.shape[0]
    blk = lambda lo, hi: slice(lo * BLOCK, (hi + 1) * BLOCK)
    for g, (_, d) in enumerate(ATTN_PATTERNS):
        if d == 1:
            continue
        q_ref, k_ref, v_ref = refs[g]
        for r in range(d):
            def head(hd, g=g, d=d, r=r, q_ref=q_ref, k_ref=k_ref, v_ref=v_ref):
                def store(n, acc, l, m):
                    rows = pl.ds(n * BLOCK * _row_pitch(d) + r, BLOCK, stride=_row_pitch(d))
                    acc_scr[hd, g - 1, rows, :] = acc
                    l_scr[hd, g - 1, rows, :] = l
                    m_scr[hd, g - 1, rows, :] = m
                return (lambda n: q_ref[hd, r, blk(n, n), :], lambda lo, hi: k_ref[hd, r, blk(lo, hi), :],
                        lambda lo, hi: v_ref[hd, r, blk(lo, hi), :], store)
            subsequence([head(hd) for hd in range(n_heads)], seq // d // BLOCK)

    q_ref, k_ref, v_ref = refs[0]

    def head(hd):
        def merge(n, acc, l, m):
            rows = pl.ds(n * BLOCK, BLOCK)

            def tokens(scr, g):
                d = ATTN_PATTERNS[g + 1][1]
                if _row_pitch(d) == d:
                    return scr[hd, g, rows, :]
                return jnp.concatenate([scr[hd, g, pl.ds(_row_pitch(d) * j, d), :]
                                        for j in range(n * BLOCK // d, (n + 1) * BLOCK // d)], axis=0)

            accs = [acc] + [tokens(acc_scr, g) for g in range(n_groups - 1)]
            ls = [l] + [tokens(l_scr, g) for g in range(n_groups - 1)]
            ms = [m] + [tokens(m_scr, g) for g in range(n_groups - 1)]
            mx = functools.reduce(jnp.maximum, ms)
            ws = [jnp.exp2(mi - mx) for mi in ms]
            num = sum(w * a for w, a in zip(ws, accs))
            den = sum(w * li for w, li in zip(ws, ls))
            o_ref[hd, rows, :] = (num / den).astype(BF16)
        return (lambda n: q_ref[hd, blk(n, n), :], lambda lo, hi: k_ref[hd, blk(lo, hi), :],
                lambda lo, hi: v_ref[hd, blk(lo, hi), :], merge)
    subsequence([head(hd) for hd in range(n_heads)], seq // BLOCK)


def _attention(qkv, seq):
    B = qkv[0].shape[0]
    nh = ATTN_HEADS_PER_STEP
    in_specs = []
    for _, d in _qkv_patterns():
        if d == 1:
            in_specs.append(pl.BlockSpec((None, nh, seq, HEAD_DIM), lambda b, h: (b, h, 0, 0)))
        else:
            in_specs.append(pl.BlockSpec((None, nh, d, seq // d, HEAD_DIM),
                                         lambda b, h: (b, h, 0, 0, 0)))
    n_dilated = len(ATTN_PATTERNS) - 1
    return pl.pallas_call(
        functools.partial(_attn_kernel, seq=seq),
        grid=(B, HEADS_PER_GROUP // nh),
        in_specs=in_specs,
        out_specs=pl.BlockSpec((None, nh, seq, HEAD_DIM), lambda b, h: (b, h, 0, 0)),
        out_shape=jax.ShapeDtypeStruct((B, HEADS_PER_GROUP, seq, HEAD_DIM), BF16),
        scratch_shapes=[pltpu.VMEM(
            (nh, n_dilated, max(_pitched_rows(seq, d) for _, d in ATTN_PATTERNS), HEAD_DIM), F32)] * 3,
        compiler_params=_PARAMS,
        name="attention",
    )(*qkv)


def _post_kernel(x_ref, attn_ref, p_ref, gmix_ref, wside_ref,
                 wya_ref, poolw_ref, pscale_ref, wyb_ref, wo_ref, gffn_ref, wup_ref, cw_ref, cb_ref,
                 wdown_ref, gple_ref, wple_ref, wpleg_ref, gfin_ref, out_ref,
                 pool_halo, u_scr, *, tm, final):
    i = pl.program_id(1)
    d_ff = wdown_ref.shape[0]
    d_model = x_ref.shape[-1]
    x = x_ref[...]

    @pl.when(i == 0)
    def _():
        pool_halo[...] = jnp.zeros(pool_halo.shape, F32)
        u_scr[0:CONV_HALO, :] = jnp.zeros((CONV_HALO, u_scr.shape[1]), F32)

    attn = jnp.concatenate([attn_ref[hh] for hh in range(HEADS_PER_GROUP)], axis=1)
    y_a = _dot(attn, wya_ref[...])

    side = _dot(_rms_scale(x, gmix_ref[...]).astype(BF16), wside_ref[...])
    u = side[:, :POOL_WIDTH]
    gate_a = side[:, POOL_WIDTH:POOL_WIDTH + d_model]
    gate_b = side[:, POOL_WIDTH + d_model:]

    prev = pool_halo[...]
    pool_halo[...] = u[tm - POOL_HALO:, :]
    tok = i * tm + lax.broadcasted_iota(jnp.int32, (tm, 1), 0)
    mixed = []
    for g, w in enumerate(POOL_WINDOWS):
        cols = slice(g * POOL_GROUP_WIDTH, (g + 1) * POOL_GROUP_WIDTH)
        ug = u[:, cols]
        s = jnp.concatenate([prev[:, cols], ug], axis=0)
        shift = 1
        while shift < w:
            s = s + pltpu.roll(s, shift, 0)
            shift *= 2
        count = jnp.minimum(tok + 1, w).astype(F32)
        pooled = s[POOL_HALO:] / count - ug
        mixed.append(_dot(pooled.astype(BF16), poolw_ref[g]))
    mixed = jnp.concatenate(mixed, axis=1) * pscale_ref[...]
    y_b = _dot(mixed.astype(BF16), wyb_ref[...])

    merged = jax.nn.sigmoid(gate_a) * y_a + jax.nn.sigmoid(gate_b) * y_b
    x = x + _dot(merged.astype(BF16), wo_ref[...])

    h = _rms_scale(x, gffn_ref[...]).astype(BF16)
    u_scr[CONV_HALO:CONV_HALO + tm, :] = _dot(h, wup_ref[...])

    def conv(cols):
        y = cb_ref[:, cols]
        for tap in range(CONV_WIDTH):
            lo = CONV_HALO - (CONV_WIDTH - 1 - tap)
            y = y + cw_ref[tap:tap + 1, cols] * u_scr[lo:lo + tm, cols]
        return y

    acts = []
    for j in range(d_ff // FF_CHUNK):
        gate = conv(slice(j * FF_CHUNK, (j + 1) * FF_CHUNK))
        val = conv(slice(d_ff + j * FF_CHUNK, d_ff + (j + 1) * FF_CHUNK))
        acts.append((gate * jax.nn.sigmoid(gate) * val).astype(BF16))
    u_scr[0:CONV_HALO, :] = u_scr[tm:tm + CONV_HALO, :]
    x = x + _dot(jnp.concatenate(acts, axis=1), wdown_ref[...])

    h = _rms_scale(x, gple_ref[...]).astype(BF16)
    gate = jax.nn.sigmoid(_dot(h, wpleg_ref[...]))
    x = x + _dot(p_ref[...].astype(BF16), wple_ref[...]) * gate
    if final:
        x = _rms_scale(x, gfin_ref[...])
    out_ref[...] = x


def _post(x, attn, p, weights, g_final, layer, tm, final):
    B, S, D = x.shape
    assert S % tm == 0 and tm >= POOL_HALO
    d_up = weights[8].shape[-1]
    tok = pl.BlockSpec((None, tm, D), lambda b, i: (b, i, 0))
    return pl.pallas_call(
        functools.partial(_post_kernel, tm=tm, final=final),
        grid=(B, S // tm),
        in_specs=[tok,
                  pl.BlockSpec((None, HEADS_PER_GROUP, tm, HEAD_DIM), lambda b, i: (b, 0, i, 0)),
                  pl.BlockSpec((None, None, tm, p.shape[-1]), lambda b, i: (layer, b, i, 0))]
        + [_layer_block(w, layer) for w in weights]
        + [pl.BlockSpec(g_final.shape, lambda b, i: (0, 0), pipeline_mode=pl.Buffered(1))],
        out_specs=tok,
        out_shape=jax.ShapeDtypeStruct((B, S, D), F32),
        scratch_shapes=[pltpu.VMEM((POOL_HALO, POOL_WIDTH), F32),
                        pltpu.VMEM((tm + CONV_HALO, d_up), F32)],
        compiler_params=_PARAMS,
        name="post",
    )(x, attn, p, *weights, g_final)


def _rope_column_order():
    return ((0, ROPE_HALF), (ROPE_PARTNER, ROPE_PARTNER + ROPE_HALF), (ROPE_DIM, ROPE_PARTNER),
            (ROPE_HALF, ROPE_DIM), (ROPE_PARTNER + ROPE_HALF, HEAD_DIM))


def _rotary_tables(seq):
    pos = jnp.arange(seq, dtype=F32)
    inv_freq = jnp.exp(jnp.arange(0, ROPE_DIM, 2, dtype=F32) * (-math.log(ROPE_THETA) / ROPE_DIM))
    ang = pos[:, None] * inv_freq[None, :]
    cos, sin = jnp.cos(ang), jnp.sin(ang)
    gap = jnp.zeros((seq, ROPE_PARTNER - ROPE_HALF), F32)
    c = jnp.concatenate([cos, gap + 1.0, cos, gap + 1.0], axis=1)
    s = jnp.concatenate([-sin, gap, sin, gap], axis=1)
    return (c, s)


def kernel(x, p, g_mix, w_in, w_ya, w_yb, pool_w, pool_scale, w_o, g_ffn, w_up, conv_w, conv_b,
           w_down, g_ple, w_ple, w_ple_gate, g_final):
    B, S, D = x.shape
    depth = w_in.shape[0]
    assert S % QKV_TOKENS == 0 and HEADS_PER_GROUP % ATTN_HEADS_PER_STEP == 0
    for window, d in ATTN_PATTERNS:
        assert window // d == BLOCK and S % (BLOCK * d) == 0
        assert QKV_TOKENS % (16 * d) == 0
    assert ATTN_PATTERNS[0][1] == 1 and all(d > 1 for _, d in ATTN_PATTERNS[1:])
    tables = _rotary_tables(S)
    rows = lambda a: a.reshape(depth, 1, a.shape[-1])
    g_mix = rows(g_mix)
    qk = w_in[:, :, :2 * ATTN_WIDTH].reshape(depth, D, -1, HEAD_DIM)
    qk = jnp.concatenate([qk[..., lo:hi] for lo, hi in _rope_column_order()], axis=-1)
    w_qkv = jnp.concatenate([qk.reshape(depth, D, 2 * ATTN_WIDTH),
                             w_in[:, :, 2 * ATTN_WIDTH:3 * ATTN_WIDTH]], axis=-1).astype(BF16)
    w_side = w_in[:, :, 3 * ATTN_WIDTH:].astype(BF16)
    post_weights = (g_mix, w_side,
                    w_ya.astype(BF16), pool_w.astype(BF16), rows(pool_scale), w_yb.astype(BF16),
                    w_o.astype(BF16), rows(g_ffn), w_up.astype(BF16), conv_w, rows(conv_b),
                    w_down.astype(BF16), rows(g_ple), w_ple.astype(BF16), w_ple_gate.astype(BF16))
    g_final = g_final.reshape(1, D)
    for layer in range(depth):
        qkv = _qkv(x, g_mix, w_qkv, tables, layer, QKV_TOKENS)
        attn = _attention(qkv, S)
        x = _post(x, attn, p, post_weights, g_final, layer, POST_TOKENS, final=(layer == depth - 1))
    return x
```

```python
import functools
import math

import jax
import jax.numpy as jnp
from jax import lax
from jax.experimental import pallas as pl
from jax.experimental.pallas import tpu as pltpu

F32 = jnp.float32
BF16 = jnp.bfloat16

HEAD_DIM = 128
HEADS_PER_GROUP = 4
GROUP_WIDTH = HEADS_PER_GROUP * HEAD_DIM
ATTN_PATTERNS = ((128, 1), (512, 4), (2048, 16))
ATTN_WIDTH = len(ATTN_PATTERNS) * GROUP_WIDTH
ROPE_DIM = HEAD_DIM // 4
ROPE_HALF = ROPE_DIM // 2
ROPE_THETA = 500000.0
BLOCK = 128
NEG_INF = -1e30
POOL_WINDOWS = (2, 4, 8, 16)
POOL_GROUP_WIDTH = 256
POOL_WIDTH = len(POOL_WINDOWS) * POOL_GROUP_WIDTH
POOL_HALO = 16
CONV_WIDTH = 3
CONV_HALO = 8
RMS_EPS = 1e-6
Q_SCALE = HEAD_DIM ** -0.5 * math.log2(math.e)

VMEM_LIMIT_BYTES = 58 * 1024 * 1024
QKV_TOKENS = 1024
POST_TOKENS = 512
POST_SUB_TOKENS = 256
FF_CHUNK = 256
ATTN_HEADS_PER_STEP = 2
PERM_SLABS = 8

_PARAMS = pltpu.CompilerParams(
    dimension_semantics=("arbitrary", "arbitrary"), vmem_limit_bytes=VMEM_LIMIT_BYTES)


def _rms_scale(x, g):
    ms = jnp.mean(x * x, axis=-1, keepdims=True)
    return x * lax.rsqrt(ms + RMS_EPS) * g


def _layer_block(arr, layer):
    tail = (0,) * (arr.ndim - 1)
    return pl.BlockSpec((None,) + arr.shape[1:], lambda *_: (layer,) + tail,
                        pipeline_mode=pl.Buffered(1))


def _dot(a, b):
    return jnp.dot(a, b, preferred_element_type=F32)


def _qkv_patterns():
    return [pat for pat in ATTN_PATTERNS for _ in range(3)]


def _row_pitch(d):
    return d + 1 if d % 8 == 0 else d


def _pitched_rows(n, d):
    return n // d * _row_pitch(d)


def _qkv_kernel(x_ref, g_ref, w_ref, c_ref, s_ref, *rest, tm):
    qkv_refs, perm_scr = rest[:-1], rest[-1]
    h = _rms_scale(x_ref[...], g_ref[...]).astype(BF16)
    first_half = lax.broadcasted_iota(jnp.int32, (tm, HEAD_DIM), 1) < ROPE_HALF

    def rotary(t, scale):
        c, s = c_ref[...] * scale, s_ref[...] * scale
        heads = []
        for hh in range(HEADS_PER_GROUP):
            th = t[:, hh * HEAD_DIM:(hh + 1) * HEAD_DIM]
            partner = jnp.where(first_half, pltpu.roll(th, HEAD_DIM - ROPE_HALF, 1),
                                pltpu.roll(th, ROPE_HALF, 1))
            heads.append(th * c + partner * s)
        return jnp.concatenate(heads, axis=1)

    ring = [0]

    def put(dst_ref, val, d):
        for hh in range(HEADS_PER_GROUP):
            head = val[:, hh * HEAD_DIM:(hh + 1) * HEAD_DIM]
            if d == 1:
                dst_ref[hh] = head.astype(BF16)
                continue
            scr = perm_scr.at[ring[0] % perm_scr.shape[0]]
            ring[0] += 1
            pitch = _row_pitch(d)
            if pitch == d:
                scr[0:tm, :] = head
            else:
                for j in range(tm // d):
                    scr[pl.ds(pitch * j, d), :] = head[d * j:d * (j + 1), :]
            for r in range(d):
                dst_ref[hh, r] = scr[pl.ds(r, tm // d, stride=pitch), :].astype(BF16)

    for g, (_, d) in reversed(list(enumerate(ATTN_PATTERNS))):
        q_ref, k_ref, v_ref = qkv_refs[3 * g:3 * g + 3]
        col = lambda j: slice(j * ATTN_WIDTH + g * GROUP_WIDTH, j * ATTN_WIDTH + (g + 1) * GROUP_WIDTH)
        put(q_ref, rotary(_dot(h, w_ref[:, col(0)]), Q_SCALE), d)
        put(k_ref, rotary(_dot(h, w_ref[:, col(1)]), 1.0), d)
        put(v_ref, _dot(h, w_ref[:, col(2)]), d)


def _qkv(x, g_mix, w_qkv, tables, layer, tm):
    B, S, D = x.shape
    tab = pl.BlockSpec((tm, HEAD_DIM), lambda b, i: (i, 0))
    out_shape, out_specs = [], []
    for _, d in _qkv_patterns():
        if d == 1:
            out_shape.append(jax.ShapeDtypeStruct((B, HEADS_PER_GROUP, S, HEAD_DIM), BF16))
            out_specs.append(pl.BlockSpec((None, HEADS_PER_GROUP, tm, HEAD_DIM),
                                          lambda b, i: (b, 0, i, 0)))
        else:
            out_shape.append(jax.ShapeDtypeStruct((B, HEADS_PER_GROUP, d, S // d, HEAD_DIM), BF16))
            out_specs.append(pl.BlockSpec((None, HEADS_PER_GROUP, d, tm // d, HEAD_DIM),
                                          lambda b, i: (b, 0, 0, i, 0)))
    return pl.pallas_call(
        functools.partial(_qkv_kernel, tm=tm),
        grid=(B, S // tm),
        in_specs=[pl.BlockSpec((None, tm, D), lambda b, i: (b, i, 0)),
                  _layer_block(g_mix, layer), _layer_block(w_qkv, layer), tab, tab],
        out_specs=out_specs,
        out_shape=out_shape,
        scratch_shapes=[pltpu.VMEM(
            (PERM_SLABS, max(_pitched_rows(tm, d) for _, d in ATTN_PATTERNS), HEAD_DIM), F32)],
        compiler_params=_PARAMS,
        name="qkv",
    )(x, g_mix, w_qkv, *tables)


def _attn_kernel(*args, seq):
    n_groups = len(ATTN_PATTERNS)
    refs = [args[3 * g:3 * g + 3] for g in range(n_groups)]
    o_ref, acc_scr, l_scr, m_scr = args[3 * n_groups:]
    row = lax.broadcasted_iota(jnp.int32, (BLOCK, 2 * BLOCK), 0)
    col = lax.broadcasted_iota(jnp.int32, (BLOCK, 2 * BLOCK), 1)
    bias_two = jnp.where(col < BLOCK, jnp.where(col >= row, 0.0, NEG_INF),
                         jnp.where(col - BLOCK <= row, 0.0, NEG_INF)).astype(F32)
    row1 = lax.broadcasted_iota(jnp.int32, (BLOCK, BLOCK), 0)
    col1 = lax.broadcasted_iota(jnp.int32, (BLOCK, BLOCK), 1)
    bias_one = jnp.where(col1 <= row1, 0.0, NEG_INF).astype(F32)

    def subsequence(heads, n_blocks):
        for n in range(n_blocks):
            for q_at, k_at, v_at, store in heads:
                lo, bias = (0, bias_one) if n == 0 else (n - 1, bias_two)
                kk, vv = k_at(lo, n), v_at(lo, n)
                s = lax.dot_general(q_at(n), kk, (((1,), (1,)), ((), ())),
                                    preferred_element_type=F32) + bias
                m = jnp.max(s, axis=-1, keepdims=True)
                p = jnp.exp2(s - m).astype(BF16)
                vext = jnp.concatenate([vv, jnp.ones((vv.shape[0], HEAD_DIM), BF16)], axis=1)
                o = _dot(p, vext)
                store(n, o[:, :HEAD_DIM], o[:, HEAD_DIM:], jnp.broadcast_to(m, (BLOCK, HEAD_DIM)))

    n_heads = o_ref.shape[0]
    blk = lambda lo, hi: slice(lo * BLOCK, (hi + 1) * BLOCK)
    for g, (_, d) in enumerate(ATTN_PATTERNS):
        if d == 1:
            continue
        q_ref, k_ref, v_ref = refs[g]
        for r in range(d):
            def head(hd, g=g, d=d, r=r, q_ref=q_ref, k_ref=k_ref, v_ref=v_ref):
                def store(n, acc, l, m):
                    rows = pl.ds(n * BLOCK * _row_pitch(d) + r, BLOCK, stride=_row_pitch(d))
                    acc_scr[hd, g - 1, rows, :] = acc
                    l_scr[hd, g - 1, rows, :] = l
                    m_scr[hd, g - 1, rows, :] = m
                return (lambda n: q_ref[hd, r, blk(n, n), :], lambda lo, hi: k_ref[hd, r, blk(lo, hi), :],
                        lambda lo, hi: v_ref[hd, r, blk(lo, hi), :], store)
            subsequence([head(hd) for hd in range(n_heads)], seq // d // BLOCK)

    q_ref, k_ref, v_ref = refs[0]

    def head(hd):
        def merge(n, acc, l, m):
            rows = pl.ds(n * BLOCK, BLOCK)

            def tokens(scr, g):
                d = ATTN_PATTERNS[g + 1][1]
                if _row_pitch(d) == d:
                    return scr[hd, g, rows, :]
                return jnp.concatenate([scr[hd, g, pl.ds(_row_pitch(d) * j, d), :]
                                        for j in range(n * BLOCK // d, (n + 1) * BLOCK // d)], axis=0)

            accs = [acc] + [tokens(acc_scr, g) for g in range(n_groups - 1)]
            ls = [l] + [tokens(l_scr, g) for g in range(n_groups - 1)]
            ms = [m] + [tokens(m_scr, g) for g in range(n_groups - 1)]
            mx = functools.reduce(jnp.maximum, ms)
            ws = [jnp.exp2(mi - mx) for mi in ms]
            num = sum(w * a for w, a in zip(ws, accs))
            den = sum(w * li for w, li in zip(ws, ls))
            o_ref[hd, rows, :] = (num / den).astype(BF16)
        return (lambda n: q_ref[hd, blk(n, n), :], lambda lo, hi: k_ref[hd, blk(lo, hi), :],
                lambda lo, hi: v_ref[hd, blk(lo, hi), :], merge)
    subsequence([head(hd) for hd in range(n_heads)], seq // BLOCK)


def _attention(qkv, seq):
    B = qkv[0].shape[0]
    nh = ATTN_HEADS_PER_STEP
    in_specs = []
    for _, d in _qkv_patterns():
        if d == 1:
            in_specs.append(pl.BlockSpec((None, nh, seq, HEAD_DIM), lambda b, h: (b, h, 0, 0)))
        else:
            in_specs.append(pl.BlockSpec((None, nh, d, seq // d, HEAD_DIM),
                                         lambda b, h: (b, h, 0, 0, 0)))
    n_dilated = len(ATTN_PATTERNS) - 1
    return pl.pallas_call(
        functools.partial(_attn_kernel, seq=seq),
        grid=(B, HEADS_PER_GROUP // nh),
        in_specs=in_specs,
        out_specs=pl.BlockSpec((None, nh, seq, HEAD_DIM), lambda b, h: (b, h, 0, 0)),
        out_shape=jax.ShapeDtypeStruct((B, HEADS_PER_GROUP, seq, HEAD_DIM), BF16),
        scratch_shapes=[pltpu.VMEM(
            (nh, n_dilated, max(_pitched_rows(seq, d) for _, d in ATTN_PATTERNS), HEAD_DIM), F32)] * 3,
        compiler_params=_PARAMS,
        name="attention",
    )(*qkv)


def _post_kernel(x_ref, attn_ref, p_ref, gmix_ref, wside_ref,
                 wya_ref, poolw_ref, pscale_ref, wyb_ref, wo_ref, gffn_ref, wup_ref, cw_ref, cb_ref,
                 wdown_ref, gple_ref, wple_ref, wpleg_ref, gfin_ref, out_ref,
                 pool_halo, u_scr, *, tm, sub, final):
    i = pl.program_id(1)
    d_ff = wdown_ref.shape[0]
    d_model = x_ref.shape[-1]

    @pl.when(i == 0)
    def _():
        pool_halo[...] = jnp.zeros(pool_halo.shape, F32)
        u_scr[0:CONV_HALO, :] = jnp.zeros((CONV_HALO, u_scr.shape[1]), F32)

    def chain(c, carry):
        r0 = pl.multiple_of(c * sub, sub)
        rows = pl.ds(r0, sub)
        x = x_ref[rows, :]

        attn = jnp.concatenate([attn_ref[hh, rows, :] for hh in range(HEADS_PER_GROUP)], axis=1)
        y_a = _dot(attn, wya_ref[...])

        side = _dot(_rms_scale(x, gmix_ref[...]).astype(BF16), wside_ref[...])
        u = side[:, :POOL_WIDTH]
        gate_a = side[:, POOL_WIDTH:POOL_WIDTH + d_model]
        gate_b = side[:, POOL_WIDTH + d_model:]

        prev = pool_halo[...]
        pool_halo[...] = u[sub - POOL_HALO:, :]
        tok = i * tm + r0 + lax.broadcasted_iota(jnp.int32, (sub, 1), 0)
        mixed = []
        for g, w in enumerate(POOL_WINDOWS):
            cols = slice(g * POOL_GROUP_WIDTH, (g + 1) * POOL_GROUP_WIDTH)
            ug = u[:, cols]
            s = jnp.concatenate([prev[:, cols], ug], axis=0)
            shift = 1
            while shift < w:
                s = s + pltpu.roll(s, shift, 0)
                shift *= 2
            count = jnp.minimum(tok + 1, w).astype(F32)
            pooled = s[POOL_HALO:] / count - ug
            mixed.append(_dot(pooled.astype(BF16), poolw_ref[g]))
        mixed = jnp.concatenate(mixed, axis=1) * pscale_ref[...]
        y_b = _dot(mixed.astype(BF16), wyb_ref[...])

        merged = jax.nn.sigmoid(gate_a) * y_a + jax.nn.sigmoid(gate_b) * y_b
        x = x + _dot(merged.astype(BF16), wo_ref[...])

        h = _rms_scale(x, gffn_ref[...]).astype(BF16)
        u_scr[CONV_HALO:CONV_HALO + sub, :] = _dot(h, wup_ref[...])

        def conv(cols):
            y = cb_ref[:, cols]
            for tap in range(CONV_WIDTH):
                lo = CONV_HALO - (CONV_WIDTH - 1 - tap)
                y = y + cw_ref[tap:tap + 1, cols] * u_scr[lo:lo + sub, cols]
            return y

        acts = []
        for j in range(d_ff // FF_CHUNK):
            gate = conv(slice(j * FF_CHUNK, (j + 1) * FF_CHUNK))
            val = conv(slice(d_ff + j * FF_CHUNK, d_ff + (j + 1) * FF_CHUNK))
            acts.append((gate * jax.nn.sigmoid(gate) * val).astype(BF16))
        u_scr[0:CONV_HALO, :] = u_scr[sub:sub + CONV_HALO, :]
        x = x + _dot(jnp.concatenate(acts, axis=1), wdown_ref[...])

        h = _rms_scale(x, gple_ref[...]).astype(BF16)
        gate = jax.nn.sigmoid(_dot(h, wpleg_ref[...]))
        x = x + _dot(p_ref[rows, :].astype(BF16), wple_ref[...]) * gate
        if final:
            x = _rms_scale(x, gfin_ref[...])
        out_ref[rows, :] = x
        return carry

    lax.fori_loop(0, tm // sub, chain, 0)


def _post(x, attn, p, weights, g_final, layer, tm, final):
    B, S, D = x.shape
    assert S % tm == 0 and tm % POST_SUB_TOKENS == 0 and POST_SUB_TOKENS >= POOL_HALO
    d_up = weights[8].shape[-1]
    tok = pl.BlockSpec((None, tm, D), lambda b, i: (b, i, 0))
    return pl.pallas_call(
        functools.partial(_post_kernel, tm=tm, sub=POST_SUB_TOKENS, final=final),
        grid=(B, S // tm),
        in_specs=[tok,
                  pl.BlockSpec((None, HEADS_PER_GROUP, tm, HEAD_DIM), lambda b, i: (b, 0, i, 0)),
                  pl.BlockSpec((None, None, tm, p.shape[-1]), lambda b, i: (layer, b, i, 0))]
        + [_layer_block(w, layer) for w in weights]
        + [pl.BlockSpec(g_final.shape, lambda b, i: (0, 0), pipeline_mode=pl.Buffered(1))],
        out_specs=tok,
        out_shape=jax.ShapeDtypeStruct((B, S, D), F32),
        scratch_shapes=[pltpu.VMEM((POOL_HALO, POOL_WIDTH), F32),
                        pltpu.VMEM((POST_SUB_TOKENS + CONV_HALO, d_up), F32)],
        compiler_params=_PARAMS,
        name="post",
    )(x, attn, p, *weights, g_final)


def _rotary_tables(seq):
    pos = jnp.arange(seq, dtype=F32)
    inv_freq = jnp.exp(jnp.arange(0, ROPE_DIM, 2, dtype=F32) * (-math.log(ROPE_THETA) / ROPE_DIM))
    ang = pos[:, None] * inv_freq[None, :]
    cos, sin = jnp.cos(ang), jnp.sin(ang)
    rest = jnp.zeros((seq, HEAD_DIM - ROPE_DIM), F32)
    c = jnp.concatenate([cos, cos, rest + 1.0], axis=1)
    s = jnp.concatenate([-sin, sin, rest], axis=1)
    return (c, s)


def kernel(x, p, g_mix, w_in, w_ya, w_yb, pool_w, pool_scale, w_o, g_ffn, w_up, conv_w, conv_b,
           w_down, g_ple, w_ple, w_ple_gate, g_final):
    B, S, D = x.shape
    depth = w_in.shape[0]
    assert S % QKV_TOKENS == 0 and HEADS_PER_GROUP % ATTN_HEADS_PER_STEP == 0
    for window, d in ATTN_PATTERNS:
        assert window // d == BLOCK and S % (BLOCK * d) == 0
        assert QKV_TOKENS % (16 * d) == 0
    assert ATTN_PATTERNS[0][1] == 1 and all(d > 1 for _, d in ATTN_PATTERNS[1:])
    tables = _rotary_tables(S)
    rows = lambda a: a.reshape(depth, 1, a.shape[-1])
    g_mix = rows(g_mix)
    w_qkv = w_in[:, :, :3 * ATTN_WIDTH].astype(BF16)
    w_side = w_in[:, :, 3 * ATTN_WIDTH:].astype(BF16)
    post_weights = (g_mix, w_side,
                    w_ya.astype(BF16), pool_w.astype(BF16), rows(pool_scale), w_yb.astype(BF16),
                    w_o.astype(BF16), rows(g_ffn), w_up.astype(BF16), conv_w, rows(conv_b),
                    w_down.astype(BF16), rows(g_ple), w_ple.astype(BF16), w_ple_gate.astype(BF16))
    g_final = g_final.reshape(1, D)
    for layer in range(depth):
        qkv = _qkv(x, g_mix, w_qkv, tables, layer, QKV_TOKENS)
        attn = _attention(qkv, S)
        x = _post(x, attn, p, post_weights, g_final, layer, POST_TOKENS, final=(layer == depth - 1))
    return x
```

```python
import functools
import math

import jax
import jax.numpy as jnp
from jax import lax
from jax.experimental import pallas as pl
from jax.experimental.pallas import tpu as pltpu

F32 = jnp.float32
BF16 = jnp.bfloat16

HEAD_DIM = 128
HEADS_PER_GROUP = 4
GROUP_WIDTH = HEADS_PER_GROUP * HEAD_DIM
ATTN_PATTERNS = ((128, 1), (512, 4), (2048, 16))
ATTN_WIDTH = len(ATTN_PATTERNS) * GROUP_WIDTH
ROPE_DIM = HEAD_DIM // 4
ROPE_HALF = ROPE_DIM // 2
ROPE_THETA = 500000.0
BLOCK = 128
NEG_INF = -1e30
POOL_WINDOWS = (2, 4, 8, 16)
POOL_GROUP_WIDTH = 256
POOL_WIDTH = len(POOL_WINDOWS) * POOL_GROUP_WIDTH
POOL_HALO = 16
CONV_WIDTH = 3
CONV_HALO = 8
RMS_EPS = 1e-6
Q_SCALE = HEAD_DIM ** -0.5 * math.log2(math.e)

VMEM_LIMIT_BYTES = 58 * 1024 * 1024
QKV_TOKENS = 1024
POST_TOKENS = 512
POST_SUB_TOKENS = 256
FF_CHUNK = 256
ATTN_HEADS_PER_STEP = 2
PERM_SLABS = 8

_PARAMS = pltpu.CompilerParams(
    dimension_semantics=("arbitrary", "arbitrary"), vmem_limit_bytes=VMEM_LIMIT_BYTES)
POST_VMEM_LIMIT_BYTES = 63 * 1024 * 1024
_POST_PARAMS = pltpu.CompilerParams(
    dimension_semantics=("arbitrary", "arbitrary"), vmem_limit_bytes=POST_VMEM_LIMIT_BYTES)


def _rms_scale(x, g):
    ms = jnp.mean(x * x, axis=-1, keepdims=True)
    return x * lax.rsqrt(ms + RMS_EPS) * g


def _layer_block(arr, layer):
    tail = (0,) * (arr.ndim - 1)
    return pl.BlockSpec((None,) + arr.shape[1:], lambda *_: (layer,) + tail,
                        pipeline_mode=pl.Buffered(1))


def _dot(a, b):
    return jnp.dot(a, b, preferred_element_type=F32)


def _qkv_patterns():
    return [pat for pat in ATTN_PATTERNS for _ in range(3)]


def _row_pitch(d):
    return d + 1 if d % 8 == 0 else d


def _pitched_rows(n, d):
    return n // d * _row_pitch(d)


def _qkv_kernel(x_ref, g_ref, w_ref, c_ref, s_ref, *rest, tm):
    qkv_refs, perm_scr = rest[:-1], rest[-1]
    h = _rms_scale(x_ref[...], g_ref[...]).astype(BF16)
    first_half = lax.broadcasted_iota(jnp.int32, (tm, HEAD_DIM), 1) < ROPE_HALF

    def rotary(t, scale):
        c, s = c_ref[...] * scale, s_ref[...] * scale
        heads = []
        for hh in range(HEADS_PER_GROUP):
            th = t[:, hh * HEAD_DIM:(hh + 1) * HEAD_DIM]
            partner = jnp.where(first_half, pltpu.roll(th, HEAD_DIM - ROPE_HALF, 1),
                                pltpu.roll(th, ROPE_HALF, 1))
            heads.append(th * c + partner * s)
        return jnp.concatenate(heads, axis=1)

    ring = [0]

    def put(dst_ref, val, d):
        for hh in range(HEADS_PER_GROUP):
            head = val[:, hh * HEAD_DIM:(hh + 1) * HEAD_DIM]
            if d == 1:
                dst_ref[hh] = head.astype(BF16)
                continue
            scr = perm_scr.at[ring[0] % perm_scr.shape[0]]
            ring[0] += 1
            pitch = _row_pitch(d)
            if pitch == d:
                scr[0:tm, :] = head
            else:
                for j in range(tm // d):
                    scr[pl.ds(pitch * j, d), :] = head[d * j:d * (j + 1), :]
            for r in range(d):
                dst_ref[hh, r] = scr[pl.ds(r, tm // d, stride=pitch), :].astype(BF16)

    for g, (_, d) in reversed(list(enumerate(ATTN_PATTERNS))):
        q_ref, k_ref, v_ref = qkv_refs[3 * g:3 * g + 3]
        col = lambda j: slice(j * ATTN_WIDTH + g * GROUP_WIDTH, j * ATTN_WIDTH + (g + 1) * GROUP_WIDTH)
        put(q_ref, rotary(_dot(h, w_ref[:, col(0)]), Q_SCALE), d)
        put(k_ref, rotary(_dot(h, w_ref[:, col(1)]), 1.0), d)
        put(v_ref, _dot(h, w_ref[:, col(2)]), d)


def _qkv(x, g_mix, w_qkv, tables, layer, tm):
    B, S, D = x.shape
    tab = pl.BlockSpec((tm, HEAD_DIM), lambda b, i: (i, 0))
    out_shape, out_specs = [], []
    for _, d in _qkv_patterns():
        if d == 1:
            out_shape.append(jax.ShapeDtypeStruct((B, HEADS_PER_GROUP, S, HEAD_DIM), BF16))
            out_specs.append(pl.BlockSpec((None, HEADS_PER_GROUP, tm, HEAD_DIM),
                                          lambda b, i: (b, 0, i, 0)))
        else:
            out_shape.append(jax.ShapeDtypeStruct((B, HEADS_PER_GROUP, d, S // d, HEAD_DIM), BF16))
            out_specs.append(pl.BlockSpec((None, HEADS_PER_GROUP, d, tm // d, HEAD_DIM),
                                          lambda b, i: (b, 0, 0, i, 0)))
    return pl.pallas_call(
        functools.partial(_qkv_kernel, tm=tm),
        grid=(B, S // tm),
        in_specs=[pl.BlockSpec((None, tm, D), lambda b, i: (b, i, 0)),
                  _layer_block(g_mix, layer), _layer_block(w_qkv, layer), tab, tab],
        out_specs=out_specs,
        out_shape=out_shape,
        scratch_shapes=[pltpu.VMEM(
            (PERM_SLABS, max(_pitched_rows(tm, d) for _, d in ATTN_PATTERNS), HEAD_DIM), F32)],
        compiler_params=_PARAMS,
        name="qkv",
    )(x, g_mix, w_qkv, *tables)


def _attn_kernel(*args, seq):
    n_groups = len(ATTN_PATTERNS)
    refs = [args[3 * g:3 * g + 3] for g in range(n_groups)]
    o_ref, acc_scr, l_scr, m_scr = args[3 * n_groups:]
    row = lax.broadcasted_iota(jnp.int32, (BLOCK, 2 * BLOCK), 0)
    col = lax.broadcasted_iota(jnp.int32, (BLOCK, 2 * BLOCK), 1)
    bias_two = jnp.where(col < BLOCK, jnp.where(col >= row, 0.0, NEG_INF),
                         jnp.where(col - BLOCK <= row, 0.0, NEG_INF)).astype(F32)
    row1 = lax.broadcasted_iota(jnp.int32, (BLOCK, BLOCK), 0)
    col1 = lax.broadcasted_iota(jnp.int32, (BLOCK, BLOCK), 1)
    bias_one = jnp.where(col1 <= row1, 0.0, NEG_INF).astype(F32)

    def subsequence(heads, n_blocks):
        for n in range(n_blocks):
            for q_at, k_at, v_at, store in heads:
                lo, bias = (0, bias_one) if n == 0 else (n - 1, bias_two)
                kk, vv = k_at(lo, n), v_at(lo, n)
                s = lax.dot_general(q_at(n), kk, (((1,), (1,)), ((), ())),
                                    preferred_element_type=F32) + bias
                m = jnp.max(s, axis=-1, keepdims=True)
                p = jnp.exp2(s - m).astype(BF16)
                vext = jnp.concatenate([vv, jnp.ones((vv.shape[0], HEAD_DIM), BF16)], axis=1)
                o = _dot(p, vext)
                store(n, o[:, :HEAD_DIM], o[:, HEAD_DIM:], jnp.broadcast_to(m, (BLOCK, HEAD_DIM)))

    n_heads = o_ref.shape[0]
    blk = lambda lo, hi: slice(lo * BLOCK, (hi + 1) * BLOCK)
    for g, (_, d) in enumerate(ATTN_PATTERNS):
        if d == 1:
            continue
        q_ref, k_ref, v_ref = refs[g]
        for r in range(d):
            def head(hd, g=g, d=d, r=r, q_ref=q_ref, k_ref=k_ref, v_ref=v_ref):
                def store(n, acc, l, m):
                    rows = pl.ds(n * BLOCK * _row_pitch(d) + r, BLOCK, stride=_row_pitch(d))
                    acc_scr[hd, g - 1, rows, :] = acc
                    l_scr[hd, g - 1, rows, :] = l
                    m_scr[hd, g - 1, rows, :] = m
                return (lambda n: q_ref[hd, r, blk(n, n), :], lambda lo, hi: k_ref[hd, r, blk(lo, hi), :],
                        lambda lo, hi: v_ref[hd, r, blk(lo, hi), :], store)
            subsequence([head(hd) for hd in range(n_heads)], seq // d // BLOCK)

    q_ref, k_ref, v_ref = refs[0]

    def head(hd):
        def merge(n, acc, l, m):
            rows = pl.ds(n * BLOCK, BLOCK)

            def tokens(scr, g):
                d = ATTN_PATTERNS[g + 1][1]
                if _row_pitch(d) == d:
                    return scr[hd, g, rows, :]
                return jnp.concatenate([scr[hd, g, pl.ds(_row_pitch(d) * j, d), :]
                                        for j in range(n * BLOCK // d, (n + 1) * BLOCK // d)], axis=0)

            accs = [acc] + [tokens(acc_scr, g) for g in range(n_groups - 1)]
            ls = [l] + [tokens(l_scr, g) for g in range(n_groups - 1)]
            ms = [m] + [tokens(m_scr, g) for g in range(n_groups - 1)]
            mx = functools.reduce(jnp.maximum, ms)
            ws = [jnp.exp2(mi - mx) for mi in ms]
            num = sum(w * a for w, a in zip(ws, accs))
            den = sum(w * li for w, li in zip(ws, ls))
            o_ref[hd, rows, :] = (num / den).astype(BF16)
        return (lambda n: q_ref[hd, blk(n, n), :], lambda lo, hi: k_ref[hd, blk(lo, hi), :],
                lambda lo, hi: v_ref[hd, blk(lo, hi), :], merge)
    subsequence([head(hd) for hd in range(n_heads)], seq // BLOCK)


def _attention(qkv, seq):
    B = qkv[0].shape[0]
    nh = ATTN_HEADS_PER_STEP
    in_specs = []
    for _, d in _qkv_patterns():
        if d == 1:
            in_specs.append(pl.BlockSpec((None, nh, seq, HEAD_DIM), lambda b, h: (b, h, 0, 0)))
        else:
            in_specs.append(pl.BlockSpec((None, nh, d, seq // d, HEAD_DIM),
                                         lambda b, h: (b, h, 0, 0, 0)))
    n_dilated = len(ATTN_PATTERNS) - 1
    return pl.pallas_call(
        functools.partial(_attn_kernel, seq=seq),
        grid=(B, HEADS_PER_GROUP // nh),
        in_specs=in_specs,
        out_specs=pl.BlockSpec((None, nh, seq, HEAD_DIM), lambda b, h: (b, h, 0, 0)),
        out_shape=jax.ShapeDtypeStruct((B, HEADS_PER_GROUP, seq, HEAD_DIM), BF16),
        scratch_shapes=[pltpu.VMEM(
            (nh, n_dilated, max(_pitched_rows(seq, d) for _, d in ATTN_PATTERNS), HEAD_DIM), F32)] * 3,
        compiler_params=_PARAMS,
        name="attention",
    )(*qkv)


def _post_kernel(x_ref, attn_ref, p_ref, gmix_ref, wside_ref,
                 wya_ref, poolw_ref, pscale_ref, wyb_ref, wo_ref, gffn_ref, wup_ref, cw_ref, cb_ref,
                 wdown_ref, gple_ref, wple_ref, wpleg_ref, gfin_ref, out_ref,
                 pool_halo, u_scr, *, tm, sub, final):
    i = pl.program_id(1)
    d_ff = wdown_ref.shape[0]
    d_model = x_ref.shape[-1]

    n_chains = tm // sub
    @pl.when(i == 0)
    def _():
        pool_halo[...] = jnp.zeros(pool_halo.shape, F32)
        u_scr[n_chains - 1, sub:sub + CONV_HALO, :] = jnp.zeros((CONV_HALO, u_scr.shape[2]), F32)

    def stage_in(st):
        rows = pl.ds(st["r0"], sub)
        st["x"] = x_ref[rows, :]
        attn = jnp.concatenate([attn_ref[hh, rows, :] for hh in range(HEADS_PER_GROUP)], axis=1)
        st["y_a"] = _dot(attn, wya_ref[...])
        side = _dot(_rms_scale(st["x"], gmix_ref[...]).astype(BF16), wside_ref[...])
        st["u"] = side[:, :POOL_WIDTH]
        st["gate_a"] = side[:, POOL_WIDTH:POOL_WIDTH + d_model]
        st["gate_b"] = side[:, POOL_WIDTH + d_model:]

    def stage_pool(st):
        u = st.pop("u")
        prev = pool_halo[...]
        pool_halo[...] = u[sub - POOL_HALO:, :]
        tok = i * tm + st["r0"] + lax.broadcasted_iota(jnp.int32, (sub, 1), 0)
        mixed = []
        for g, w in enumerate(POOL_WINDOWS):
            cols = slice(g * POOL_GROUP_WIDTH, (g + 1) * POOL_GROUP_WIDTH)
            ug = u[:, cols]
            s = jnp.concatenate([prev[:, cols], ug], axis=0)
            shift = 1
            while shift < w:
                s = s + pltpu.roll(s, shift, 0)
                shift *= 2
            count = jnp.minimum(tok + 1, w).astype(F32)
            pooled = s[POOL_HALO:] / count - ug
            mixed.append(_dot(pooled.astype(BF16), poolw_ref[g]))
        st["mixed"] = (jnp.concatenate(mixed, axis=1) * pscale_ref[...]).astype(BF16)

    def stage_merge(st):
        y_b = _dot(st.pop("mixed"), wyb_ref[...])
        merged = (jax.nn.sigmoid(st.pop("gate_a")) * st.pop("y_a")
                  + jax.nn.sigmoid(st.pop("gate_b")) * y_b)
        st["x"] = st["x"] + _dot(merged.astype(BF16), wo_ref[...])

    def stage_up(st):
        k = st["k"]
        h = _rms_scale(st["x"], gffn_ref[...]).astype(BF16)
        u_scr[k, 0:CONV_HALO, :] = u_scr[(k - 1) % n_chains, sub:sub + CONV_HALO, :]
        u_scr[k, CONV_HALO:CONV_HALO + sub, :] = _dot(h, wup_ref[...])

    def stage_down(st):
        k = st["k"]

        def conv(cols):
            y = cb_ref[:, cols]
            for tap in range(CONV_WIDTH):
                lo = CONV_HALO - (CONV_WIDTH - 1 - tap)
                y = y + cw_ref[tap:tap + 1, cols] * u_scr[k, lo:lo + sub, cols]
            return y

        acts = []
        for j in range(d_ff // FF_CHUNK):
            gate = conv(slice(j * FF_CHUNK, (j + 1) * FF_CHUNK))
            val = conv(slice(d_ff + j * FF_CHUNK, d_ff + (j + 1) * FF_CHUNK))
            acts.append((gate * jax.nn.sigmoid(gate) * val).astype(BF16))
        st["x"] = st["x"] + _dot(jnp.concatenate(acts, axis=1), wdown_ref[...])

    def stage_out(st):
        rows = pl.ds(st["r0"], sub)
        x = st["x"]
        h = _rms_scale(x, gple_ref[...]).astype(BF16)
        gate = jax.nn.sigmoid(_dot(h, wpleg_ref[...]))
        x = x + _dot(p_ref[rows, :].astype(BF16), wple_ref[...]) * gate
        if final:
            x = _rms_scale(x, gfin_ref[...])
        out_ref[rows, :] = x

    chains = [dict(k=k, r0=k * sub) for k in range(n_chains)]
    for stage in (stage_in, stage_pool, stage_merge, stage_up, stage_down, stage_out):
        for st in chains:
            stage(st)


def _post(x, attn, p, weights, g_final, layer, tm, final):
    B, S, D = x.shape
    assert S % tm == 0 and tm % POST_SUB_TOKENS == 0 and POST_SUB_TOKENS >= POOL_HALO
    d_up = weights[8].shape[-1]
    tok = pl.BlockSpec((None, tm, D), lambda b, i: (b, i, 0))
    return pl.pallas_call(
        functools.partial(_post_kernel, tm=tm, sub=POST_SUB_TOKENS, final=final),
        grid=(B, S // tm),
        in_specs=[tok,
                  pl.BlockSpec((None, HEADS_PER_GROUP, tm, HEAD_DIM), lambda b, i: (b, 0, i, 0)),
                  pl.BlockSpec((None, None, tm, p.shape[-1]), lambda b, i: (layer, b, i, 0))]
        + [_layer_block(w, layer) for w in weights]
        + [pl.BlockSpec(g_final.shape, lambda b, i: (0, 0), pipeline_mode=pl.Buffered(1))],
        out_specs=tok,
        out_shape=jax.ShapeDtypeStruct((B, S, D), F32),
        scratch_shapes=[pltpu.VMEM((POOL_HALO, POOL_WIDTH), F32),
                        pltpu.VMEM((tm // POST_SUB_TOKENS, POST_SUB_TOKENS + CONV_HALO, d_up), F32)],
        compiler_params=_POST_PARAMS,
        name="post",
    )(x, attn, p, *weights, g_final)


def _rotary_tables(seq):
    pos = jnp.arange(seq, dtype=F32)
    inv_freq = jnp.exp(jnp.arange(0, ROPE_DIM, 2, dtype=F32) * (-math.log(ROPE_THETA) / ROPE_DIM))
    ang = pos[:, None] * inv_freq[None, :]
    cos, sin = jnp.cos(ang), jnp.sin(ang)
    rest = jnp.zeros((seq, HEAD_DIM - ROPE_DIM), F32)
    c = jnp.concatenate([cos, cos, rest + 1.0], axis=1)
    s = jnp.concatenate([-sin, sin, rest], axis=1)
    return (c, s)


def kernel(x, p, g_mix, w_in, w_ya, w_yb, pool_w, pool_scale, w_o, g_ffn, w_up, conv_w, conv_b,
           w_down, g_ple, w_ple, w_ple_gate, g_final):
    B, S, D = x.shape
    depth = w_in.shape[0]
    assert S % QKV_TOKENS == 0 and HEADS_PER_GROUP % ATTN_HEADS_PER_STEP == 0
    for window, d in ATTN_PATTERNS:
        assert window // d == BLOCK and S % (BLOCK * d) == 0
        assert QKV_TOKENS % (16 * d) == 0
    assert ATTN_PATTERNS[0][1] == 1 and all(d > 1 for _, d in ATTN_PATTERNS[1:])
    tables = _rotary_tables(S)
    rows = lambda a: a.reshape(depth, 1, a.shape[-1])
    g_mix = rows(g_mix)
    w_qkv = w_in[:, :, :3 * ATTN_WIDTH].astype(BF16)
    w_side = w_in[:, :, 3 * ATTN_WIDTH:].astype(BF16)
    post_weights = (g_mix, w_side,
                    w_ya.astype(BF16), pool_w.astype(BF16), rows(pool_scale), w_yb.astype(BF16),
                    w_o.astype(BF16), rows(g_ffn), w_up.astype(BF16), conv_w, rows(conv_b),
                    w_down.astype(BF16), rows(g_ple), w_ple.astype(BF16), w_ple_gate.astype(BF16))
    g_final = g_final.reshape(1, D)
    for layer in range(depth):
        qkv = _qkv(x, g_mix, w_qkv, tables, layer, QKV_TOKENS)
        attn = _attention(qkv, S)
        x = _post(x, attn, p, post_weights, g_final, layer, POST_TOKENS, final=(layer == depth - 1))
    return x
```
